```python
import jax
import jax.numpy as jnp
from jax import lax
import numpy as np

D_MODEL = 2048
BATCH = 2
SEQ = 4096
DEPTH = 2

CHUNK = 64
Q_BLOCK = 128
SB_HEAD_DIM = 64
SB_WIDTH = D_MODEL // 2
SB_HEADS = SB_WIDTH // SB_HEAD_DIM
CONV_GROUP_DIM = 64
CONV_WIDTH = D_MODEL // 4
CONV_GROUPS = CONV_WIDTH // CONV_GROUP_DIM
CONV_KERNEL = 31
SG_HEAD_DIM = 128
SG_WIDTH = D_MODEL // 4
SG_HEADS = SG_WIDTH // SG_HEAD_DIM
SG_CHUNK = 128
D_MIX = SB_WIDTH + CONV_WIDTH + SG_WIDTH
D_IN_PROJ = 3 * SB_WIDTH + 2 * CONV_WIDTH + 2 * SG_WIDTH
D_FF = ((8 * D_MODEL // 3 + 255) // 256) * 256
FFN_CONV_KERNEL = 3
EPS = 1e-6

kernel_name = 'hybrid_sb_conformer_sgmlp_block'


def rms_norm(x, g):
    xf = x.astype(jnp.float32)
    y = xf * lax.rsqrt(jnp.mean(xf * xf, axis=-1, keepdims=True) + EPS)
    return (y * g.astype(jnp.float32)).astype(x.dtype)


def head_rms_norm(x, n_heads, g):
    shp = x.shape
    y = rms_norm(x.reshape(shp[:-1] + (n_heads, shp[-1] // n_heads)), g.reshape(n_heads, -1))
    return y.reshape(shp)


def group_layer_norm(x, g, b, n_groups):
    shp = x.shape
    xf = x.astype(jnp.float32).reshape(shp[:-1] + (n_groups, shp[-1] // n_groups))
    mu = jnp.mean(xf, axis=-1, keepdims=True)
    var = jnp.mean(jnp.square(xf - mu), axis=-1, keepdims=True)
    y = ((xf - mu) * lax.rsqrt(var + EPS)).reshape(shp)
    return (y * g.astype(jnp.float32) + b.astype(jnp.float32)).astype(x.dtype)


def causal_depthwise_conv(x, w, b):
    k, c = w.shape
    y = lax.conv_general_dilated(
        x, w[:, None, :].astype(x.dtype), window_strides=(1,), padding=[(k - 1, 0)],
        dimension_numbers=('NWC', 'WIO', 'NWC'), feature_group_count=c)
    return y + b.astype(x.dtype)


def stick_breaking_attention(q, k, v):
    s_len, dh = q.shape[1], q.shape[3]
    scale = dh ** -0.5
    outs = []
    for blk in range(s_len // Q_BLOCK):
        q0 = blk * Q_BLOCK
        kend = q0 + Q_BLOCK
        qb = q[:, q0:kend].astype(jnp.float32)
        kb = k[:, :kend].astype(jnp.float32)
        vb = v[:, :kend]
        z = jnp.einsum('bqhd,bkhd->bhqk', qb, kb) * scale
        t_idx = q0 + jnp.arange(Q_BLOCK)[:, None]
        s_idx = jnp.arange(kend)[None, :]
        causal = s_idx < t_idx
        log_beta = jax.nn.log_sigmoid(z)
        log_1m_beta = jnp.where(causal, jax.nn.log_sigmoid(-z), 0.0)
        suffix = lax.cumsum(log_1m_beta, axis=3, reverse=True) - log_1m_beta
        w = jnp.where(causal, jnp.exp(log_beta + suffix), 0.0)
        outs.append(jnp.einsum('bhqk,bkhd->bqhd', w.astype(vb.dtype), vb))
    return jnp.concatenate(outs, axis=1)


def conformer_conv(glu_in, conv_w, conv_b, ln_g, ln_b):
    a, gate = glu_in[..., :CONV_WIDTH], glu_in[..., CONV_WIDTH:]
    h = a * jax.nn.sigmoid(gate)
    h = causal_depthwise_conv(h, conv_w, conv_b)
    h = group_layer_norm(h, ln_g, ln_b, CONV_GROUPS)
    return jax.nn.silu(h)


def chunked_spatial_gating(uv, v_norm, w_s, b_s):
    bsz, s_len, _ = uv.shape
    uv = jax.nn.gelu(uv)
    u, v = uv[..., :SG_WIDTH], uv[..., SG_WIDTH:]
    v = head_rms_norm(v, SG_HEADS, v_norm)
    v = v.reshape(bsz, s_len // SG_CHUNK, SG_CHUNK, SG_HEADS, SG_HEAD_DIM)
    pos_chunk = jnp.arange(SG_CHUNK) // CHUNK
    allowed = pos_chunk[None, :] <= pos_chunk[:, None]
    w = jnp.where(allowed[None], w_s, 0.0).astype(v.dtype)
    mixed = jnp.einsum('hij,bnjhc->bnihc', w, v) + b_s.T.astype(v.dtype)[:, :, None]
    return u * mixed.reshape(bsz, s_len, SG_WIDTH)


def setup_inputs(seed: int = 0) -> dict:
    key = jax.random.key(seed)
    ks = jax.random.split(key, 19)
    f32 = jnp.float32

    def nrm(k, shape, scale):
        return jax.random.normal(k, shape, f32) * scale

    def gain(k, shape):
        return 1.0 + 0.02 * jax.random.normal(k, shape, f32)

    return {
        'x': nrm(ks[0], (BATCH, SEQ, D_MODEL), 1.0),
        'mix_norm': gain(ks[1], (DEPTH, D_MODEL)),
        'w_in': nrm(ks[2], (DEPTH, D_MODEL, D_IN_PROJ), D_MODEL ** -0.5),
        'conv_w': nrm(ks[3], (DEPTH, CONV_KERNEL, CONV_WIDTH), CONV_KERNEL ** -0.5),
        'conv_b': nrm(ks[4], (DEPTH, CONV_WIDTH), 0.02),
        'conv_ln_g': gain(ks[5], (DEPTH, CONV_WIDTH)),
        'conv_ln_b': nrm(ks[6], (DEPTH, CONV_WIDTH), 0.02),
        'sg_v_norm': gain(ks[7], (DEPTH, SG_WIDTH)),
        'sg_w': nrm(ks[8], (DEPTH, SG_HEADS, SG_CHUNK, SG_CHUNK), SG_CHUNK ** -0.5),
        'sg_b': gain(ks[9], (DEPTH, SG_HEADS, SG_CHUNK)),
        'merge_norm': gain(ks[10], (DEPTH, D_MIX)),
        'w_out': nrm(ks[11], (DEPTH, D_MIX, D_MODEL), D_MIX ** -0.5),
        'ffn_norm': gain(ks[12], (DEPTH, D_MODEL)),
        'w_gate': nrm(ks[13], (DEPTH, D_MODEL, D_FF), D_MODEL ** -0.5),
        'w_up': nrm(ks[14], (DEPTH, D_MODEL, D_FF), D_MODEL ** -0.5),
        'ffn_conv_w': nrm(ks[15], (DEPTH, FFN_CONV_KERNEL, D_FF), FFN_CONV_KERNEL ** -0.5),
        'ffn_conv_b': nrm(ks[16], (DEPTH, D_FF), 0.02),
        'w_down': nrm(ks[17], (DEPTH, D_FF, D_MODEL), D_FF ** -0.5),
        'final_norm': gain(ks[18], (D_MODEL,)),
    }


def reference(x, mix_norm, w_in, conv_w, conv_b, conv_ln_g, conv_ln_b, sg_v_norm, sg_w, sg_b,
              merge_norm, w_out, ffn_norm, w_gate, w_up, ffn_conv_w, ffn_conv_b, w_down, final_norm):
    bsz, s_len, _ = x.shape
    o_k = SB_WIDTH
    o_v = 2 * SB_WIDTH
    o_cv = 3 * SB_WIDTH
    o_sg = o_cv + 2 * CONV_WIDTH
    for l in range(DEPTH):
        h = rms_norm(x, mix_norm[l])
        proj = jnp.einsum('bsd,de->bse', h, w_in[l])
        q = proj[..., :o_k].reshape(bsz, s_len, SB_HEADS, SB_HEAD_DIM)
        k = proj[..., o_k:o_v].reshape(bsz, s_len, SB_HEADS, SB_HEAD_DIM)
        v = proj[..., o_v:o_cv].reshape(bsz, s_len, SB_HEADS, SB_HEAD_DIM)
        y_sb = stick_breaking_attention(q, k, v).reshape(bsz, s_len, SB_WIDTH)
        y_cv = conformer_conv(proj[..., o_cv:o_sg], conv_w[l], conv_b[l], conv_ln_g[l], conv_ln_b[l])
        y_sg = chunked_spatial_gating(proj[..., o_sg:], sg_v_norm[l], sg_w[l], sg_b[l])
        g = merge_norm[l]
        y = jnp.concatenate([
            head_rms_norm(y_sb, SB_HEADS, g[:SB_WIDTH]),
            head_rms_norm(y_cv, CONV_GROUPS, g[SB_WIDTH:SB_WIDTH + CONV_WIDTH]),
            head_rms_norm(y_sg, SG_HEADS, g[SB_WIDTH + CONV_WIDTH:]),
        ], axis=-1)
        x = x + jnp.einsum('bse,ed->bsd', y, w_out[l])
        h = rms_norm(x, ffn_norm[l])
        gate = causal_depthwise_conv(jnp.einsum('bsd,df->bsf', h, w_gate[l]), ffn_conv_w[l], ffn_conv_b[l])
        up = jnp.einsum('bsd,df->bsf', h, w_up[l])
        x = x + jnp.einsum('bsf,fd->bsd', jax.nn.silu(gate) * up, w_down[l])
    return rms_norm(x, final_norm)
```

```python
import functools

import jax
import jax.numpy as jnp
from jax import lax
from jax.experimental import pallas as pl
from jax.experimental.pallas import tpu as pltpu

F32 = jnp.float32
BF16 = jnp.bfloat16

EPS = 1e-6
SB_HEAD_DIM = 64
SB_HEADS = 16
SB_WIDTH = SB_HEAD_DIM * SB_HEADS
CONV_GROUP_DIM = 64
CONV_WIDTH = 512
CONV_KERNEL = 31
SG_HEAD_DIM = 128
SG_HEADS = 4
SG_WIDTH = SG_HEAD_DIM * SG_HEADS
SG_CHUNK = 128
STREAM_CHUNK = 64
FFN_CONV_KERNEL = 3

LANES = 128
SUBLANES = 8
BF16_ROWS = 16
V7X_VMEM_BYTES = 64 * 1024 * 1024


def _params(n_grid, vmem_mb):
    return pltpu.CompilerParams(
        dimension_semantics=("arbitrary",) * n_grid,
        vmem_limit_bytes=min(vmem_mb * 1024 * 1024, V7X_VMEM_BYTES - 4 * 1024 * 1024),
    )


def _rms(x, g):
    ms = jnp.mean(x * x, axis=-1, keepdims=True)
    return x * lax.rsqrt(ms + EPS) * g


def _half_lane_mean(x, low):
    s_low = jnp.sum(jnp.where(low, x, 0.0), axis=-1, keepdims=True)
    s_all = jnp.sum(x, axis=-1, keepdims=True)
    return jnp.where(low, s_low, s_all - s_low) * (1.0 / 64.0)


def _proj_kernel(x_ref, g_ref, w_ref, qkv_ref, rest_ref, h_ref, *, n_qkv_tiles):
    j = pl.program_id(1)

    @pl.when(j == 0)
    def _():
        h_ref[...] = _rms(x_ref[...], g_ref[...]).astype(BF16)

    def mm():
        return jnp.dot(h_ref[...], w_ref[...], preferred_element_type=F32)

    @pl.when(j < n_qkv_tiles)
    def _():
        qkv_ref[...] = mm().astype(BF16)

    @pl.when(j >= n_qkv_tiles)
    def _():
        rest_ref[...] = mm()


def _proj(x, gain, w, *, tm, tn):
    t, d = x.shape
    n_out = w.shape[1]
    n_qkv = 3 * SB_WIDTH
    n_rest = n_out - n_qkv
    nq = n_qkv // tn
    return pl.pallas_call(
        functools.partial(_proj_kernel, n_qkv_tiles=nq),
        grid=(t // tm, n_out // tn),
        in_specs=[
            pl.BlockSpec((tm, d), lambda i, j: (i, 0)),
            pl.BlockSpec((1, d), lambda i, j: (0, 0)),
            pl.BlockSpec((d, tn), lambda i, j: (0, j)),
        ],
        out_specs=[
            pl.BlockSpec((tm, tn), lambda i, j: (i, jnp.minimum(j, nq - 1))),
            pl.BlockSpec((tm, tn), lambda i, j: (i, jnp.maximum(j - nq, 0))),
        ],
        out_shape=[
            jax.ShapeDtypeStruct((t, n_qkv), BF16),
            jax.ShapeDtypeStruct((t, n_rest), F32),
        ],
        scratch_shapes=[pltpu.VMEM((tm, d), BF16)],
        compiler_params=_params(2, 48),
        name="rms_in_proj",
    )(x, gain.reshape(1, d), w)


def _attn_kernel(q_ref, k_ref, v_ref, g_ref, o_ref, acc_ref, car_ref, *, tq, tk):
    qi = pl.program_id(2)
    n_diag = tq // tk
    low = lax.broadcasted_iota(jnp.int32, (tq, LANES), 1) < SB_HEAD_DIM
    q = q_ref[...]
    zero = jnp.zeros_like(q)
    scale = SB_HEAD_DIM ** -0.5
    q_heads = (jnp.where(low, q, zero) * scale, jnp.where(low, zero, q) * scale)
    tri = (lax.broadcasted_iota(jnp.int32, (tk, tk), 0)
           >= lax.broadcasted_iota(jnp.int32, (tk, tk), 1)).astype(BF16)
    row = lax.broadcasted_iota(jnp.int32, (tq, tk), 0)
    col = lax.broadcasted_iota(jnp.int32, (tq, tk), 1)

    acc_ref[...] = jnp.zeros_like(acc_ref)
    car_ref[...] = jnp.zeros_like(car_ref)

    def chunk(start, causal):
        kc = k_ref[pl.ds(start, tk), :]
        vc = v_ref[pl.ds(start, tk), :]
        for h in range(2):
            z = lax.dot_general(q_heads[h], kc, (((1,), (1,)), ((), ())),
                                preferred_element_type=F32)
            sp = jnp.maximum(z, 0.0) + jnp.log(1.0 + jnp.exp(-jnp.abs(z)))
            if causal is not None:
                sp = jnp.where(causal, sp, 0.0)
            sp_hi = sp.astype(BF16)
            sp_lo = (sp - sp_hi.astype(F32)).astype(BF16)
            within = (jnp.dot(sp_hi, tri, preferred_element_type=F32)
                      + jnp.dot(sp_lo, tri, preferred_element_type=F32))
            car = car_ref[h]
            total = jnp.concatenate([car] * (tk // LANES), axis=-1) + within
            w = jnp.exp(z - total)
            if causal is not None:
                w = jnp.where(causal, w, 0.0)
            acc_ref[h] += jnp.dot(w.astype(BF16), vc, preferred_element_type=F32)
            car_ref[h] = car + jnp.sum(sp, axis=-1, keepdims=True)

    for d in range(n_diag - 1, -1, -1):
        chunk(pl.multiple_of(qi * tq + d * tk, tk), (d * tk + col) < row)

    n_off = qi * n_diag

    def body(it, carry):
        chunk(pl.multiple_of((n_off - 1 - it) * tk, tk), None)
        return carry

    lax.fori_loop(0, n_off, body, 0)

    y = jnp.where(low, acc_ref[0], acc_ref[1])
    ms = _half_lane_mean(y * y, low)
    o_ref[...] = (y * lax.rsqrt(ms + EPS) * g_ref[...]).astype(o_ref.dtype)


def _attention(qkv, gain, *, tq, tk):
    b, s, _ = qkv.shape
    n_pairs = SB_WIDTH // LANES
    return pl.pallas_call(
        functools.partial(_attn_kernel, tq=tq, tk=tk),
        grid=(b, n_pairs, s // tq),
        in_specs=[
            pl.BlockSpec((None, tq, LANES), lambda bi, hp, qi: (bi, qi, hp)),
            pl.BlockSpec((None, s, LANES), lambda bi, hp, qi: (bi, 0, n_pairs + hp)),
            pl.BlockSpec((None, s, LANES), lambda bi, hp, qi: (bi, 0, 2 * n_pairs + hp)),
            pl.BlockSpec((1, LANES), lambda bi, hp, qi: (0, hp)),
        ],
        out_specs=pl.BlockSpec((None, tq, LANES), lambda bi, hp, qi: (bi, qi, hp)),
        out_shape=jax.ShapeDtypeStruct((b, s, SB_WIDTH), BF16),
        scratch_shapes=[
            pltpu.VMEM((2, tq, LANES), F32),
            pltpu.VMEM((2, tq, LANES), F32),
        ],
        compiler_params=_params(3, 32),
        name="stick_breaking_attn",
    )(qkv, qkv, qkv, gain.reshape(1, -1))


def _conv_kernel(a_ref, gate_ref, cw_ref, cb_ref, lg_ref, lb_ref, mg_ref, o_ref, hpad_ref,
                 *, s, tr, pad):
    low = lax.broadcasted_iota(jnp.int32, (tr, LANES), 1) < CONV_GROUP_DIM
    hpad_ref[0:pad, :] = jnp.zeros((pad, LANES), F32)

    def glu(it, carry):
        r0 = pl.multiple_of(it * tr, tr)
        a = a_ref[pl.ds(r0, tr), :]
        gt = gate_ref[pl.ds(r0, tr), :]
        hpad_ref[pl.ds(pad + r0, tr), :] = a * jax.nn.sigmoid(gt)
        return carry

    lax.fori_loop(0, s // tr, glu, 0)

    cb = cb_ref[...]
    lg = lg_ref[...]
    lb = lb_ref[...]
    mg = mg_ref[...]
    win = tr + pad

    def tile(it, carry):
        r0 = pl.multiple_of(it * tr, tr)
        x = hpad_ref[pl.ds(r0, win), :]
        acc = jnp.broadcast_to(cb, (tr, LANES))
        for r in range(SUBLANES):
            xr = x if r == 0 else pltpu.roll(x, shift=win - r, axis=0)
            for a8 in range(0, win, SUBLANES):
                k = a8 + r - (pad - (CONV_KERNEL - 1))
                if 0 <= k < CONV_KERNEL and a8 + tr <= win:
                    acc = acc + xr[a8:a8 + tr, :] * cw_ref[pl.ds(k, 1), :]
        mu = _half_lane_mean(acc, low)
        dlt = acc - mu
        var = _half_lane_mean(dlt * dlt, low)
        y = dlt * lax.rsqrt(var + EPS) * lg + lb
        y = y * jax.nn.sigmoid(y)
        ms = _half_lane_mean(y * y, low)
        o_ref[pl.ds(r0, tr), :] = (y * lax.rsqrt(ms + EPS) * mg).astype(o_ref.dtype)
        return carry

    lax.fori_loop(0, s // tr, tile, 0)


def _conformer_conv(rest, cw, cb, lg, lb, mg, *, tr=128, pad=32):
    b, s, _ = rest.shape
    nblk = CONV_WIDTH // LANES
    vec = lambda: pl.BlockSpec((1, LANES), lambda bi, c: (0, c))
    return pl.pallas_call(
        functools.partial(_conv_kernel, s=s, tr=tr, pad=pad),
        grid=(b, nblk),
        in_specs=[
            pl.BlockSpec((None, s, LANES), lambda bi, c: (bi, 0, c)),
            pl.BlockSpec((None, s, LANES), lambda bi, c: (bi, 0, nblk + c)),
            pl.BlockSpec((CONV_KERNEL, LANES), lambda bi, c: (0, c)),
            vec(), vec(), vec(), vec(),
        ],
        out_specs=pl.BlockSpec((None, s, LANES), lambda bi, c: (bi, 0, c)),
        out_shape=jax.ShapeDtypeStruct((b, s, CONV_WIDTH), BF16),
        scratch_shapes=[pltpu.VMEM((s + pad, LANES), F32)],
        compiler_params=_params(2, 32),
        name="conformer_conv",
    )(rest, rest, cw, cb.reshape(1, -1), lg.reshape(1, -1), lb.reshape(1, -1), mg.reshape(1, -1))


def _sg_kernel(u_ref, v_ref, vg_ref, w_ref, b_ref, mg_ref, o_ref, *, tm):
    u = jax.nn.gelu(u_ref[...])
    v = jax.nn.gelu(v_ref[...])
    vg = vg_ref[...]
    mg = mg_ref[...]
    ri = lax.broadcasted_iota(jnp.int32, (SG_CHUNK, SG_CHUNK), 0) // STREAM_CHUNK
    ci = lax.broadcasted_iota(jnp.int32, (SG_CHUNK, SG_CHUNK), 1) // STREAM_CHUNK
    allowed = ci <= ri
    for h in range(SG_HEADS):
        sl = slice(h * SG_HEAD_DIM, (h + 1) * SG_HEAD_DIM)
        vn = _rms(v[:, sl], vg[:, sl]).astype(BF16)
        wh = jnp.where(allowed, w_ref[h], 0.0).astype(BF16)
        for n in range(tm // SG_CHUNK):
            rs = slice(n * SG_CHUNK, (n + 1) * SG_CHUNK)
            mixed = jnp.dot(wh, vn[rs], preferred_element_type=F32) + b_ref[h]
            o_ref[rs, sl] = _rms(u[rs, sl] * mixed, mg[:, sl]).astype(o_ref.dtype)


def _spatial_gating(rest, vg, w, b_full, mg, *, tm):
    t, n_rest = rest.shape
    u_blk = (n_rest - 2 * SG_WIDTH) // SG_WIDTH
    return pl.pallas_call(
        functools.partial(_sg_kernel, tm=tm),
        grid=(t // tm,),
        in_specs=[
            pl.BlockSpec((tm, SG_WIDTH), lambda i: (i, u_blk)),
            pl.BlockSpec((tm, SG_WIDTH), lambda i: (i, u_blk + 1)),
            pl.BlockSpec((1, SG_WIDTH), lambda i: (0, 0)),
            pl.BlockSpec((SG_HEADS, SG_CHUNK, SG_CHUNK), lambda i: (0, 0, 0)),
            pl.BlockSpec((SG_HEADS, SG_CHUNK, SG_HEAD_DIM), lambda i: (0, 0, 0)),
            pl.BlockSpec((1, SG_WIDTH), lambda i: (0, 0)),
        ],
        out_specs=pl.BlockSpec((tm, SG_WIDTH), lambda i: (i, 0)),
        out_shape=jax.ShapeDtypeStruct((t, SG_WIDTH), BF16),
        compiler_params=_params(1, 32),
        name="spatial_gating",
    )(rest, rest, vg.reshape(1, -1), w, b_full, mg.reshape(1, -1))


def _outproj_kernel(x_ref, ysb_ref, ycv_ref, ysg_ref, w_ref, o_ref):
    c1 = SB_WIDTH
    c2 = SB_WIDTH + CONV_WIDTH
    acc = jnp.dot(ysb_ref[...], w_ref[0:c1, :], preferred_element_type=F32)
    acc += jnp.dot(ycv_ref[...], w_ref[c1:c2, :], preferred_element_type=F32)
    acc += jnp.dot(ysg_ref[...], w_ref[c2:, :], preferred_element_type=F32)
    o_ref[...] = x_ref[...] + acc


def _out_proj(x, y_sb, y_cv, y_sg, w, *, tm, tn):
    t, d = x.shape
    k = w.shape[0]
    return pl.pallas_call(
        _outproj_kernel,
        grid=(t // tm, d // tn),
        in_specs=[
            pl.BlockSpec((tm, tn), lambda i, j: (i, j)),
            pl.BlockSpec((tm, SB_WIDTH), lambda i, j: (i, 0)),
            pl.BlockSpec((tm, CONV_WIDTH), lambda i, j: (i, 0)),
            pl.BlockSpec((tm, SG_WIDTH), lambda i, j: (i, 0)),
            pl.BlockSpec((k, tn), lambda i, j: (0, j)),
        ],
        out_specs=pl.BlockSpec((tm, tn), lambda i, j: (i, j)),
        out_shape=jax.ShapeDtypeStruct((t, d), F32),
        compiler_params=_params(2, 48),
        name="out_proj_residual",
    )(x, y_sb, y_cv, y_sg, w)


def _ffn_kernel(x_ref, halo_ref, g_ref, wg_ref, wu_ref, cw_ref, cb_ref, wd_ref, fin_ref,
                o_ref, h_ref, *, tm, tiles_per_seq, final):
    i = pl.program_id(0)
    j = pl.program_id(1)
    hr = BF16_ROWS

    @pl.when(j == 0)
    def _():
        x = x_ref[...]
        g = g_ref[...]
        h_ref[hr:, :] = _rms(x, g).astype(BF16)
        first = (i % tiles_per_seq) == 0
        hh = jnp.where(first, 0.0, _rms(halo_ref[...], g))
        h_ref[0:hr, :] = jnp.concatenate([jnp.zeros_like(hh), hh], axis=0).astype(BF16)
        o_ref[...] = x

    gate = jnp.dot(h_ref[...], wg_ref[...], preferred_element_type=F32)
    up = jnp.dot(h_ref[hr:, :], wu_ref[...], preferred_element_type=F32)
    gc = cb_ref[...] + gate[hr:, :] * cw_ref[2:3, :]
    gc = gc + gate[hr - 1:hr - 1 + tm, :] * cw_ref[1:2, :]
    gc = gc + gate[hr - 2:hr - 2 + tm, :] * cw_ref[0:1, :]
    act = (gc * jax.nn.sigmoid(gc) * up).astype(BF16)
    o_ref[...] += jnp.dot(act, wd_ref[...], preferred_element_type=F32)

    if final:
        @pl.when(j == pl.num_programs(1) - 1)
        def _():
            o_ref[...] = _rms(o_ref[...], fin_ref[...])


def _ffn(x, gain, wg, wu, cw, cb, wd, fin, *, seq, tm, fc, final):
    t, d = x.shape
    f = wg.shape[1]
    halo_blocks = tm // SUBLANES
    return pl.pallas_call(
        functools.partial(_ffn_kernel, tm=tm, tiles_per_seq=seq // tm, final=final),
        grid=(t // tm, f // fc),
        in_specs=[
            pl.BlockSpec((tm, d), lambda i, j: (i, 0)),
            pl.BlockSpec((SUBLANES, d), lambda i, j: (jnp.maximum(i * halo_blocks - 1, 0), 0)),
            pl.BlockSpec((1, d), lambda i, j: (0, 0)),
            pl.BlockSpec((d, fc), lambda i, j: (0, j)),
            pl.BlockSpec((d, fc), lambda i, j: (0, j)),
            pl.BlockSpec((FFN_CONV_KERNEL, fc), lambda i, j: (0, j)),
            pl.BlockSpec((1, fc), lambda i, j: (0, j)),
            pl.BlockSpec((fc, d), lambda i, j: (j, 0)),
            pl.BlockSpec((1, d), lambda i, j: (0, 0)),
        ],
        out_specs=pl.BlockSpec((tm, d), lambda i, j: (i, 0)),
        out_shape=jax.ShapeDtypeStruct((t, d), F32),
        scratch_shapes=[pltpu.VMEM((BF16_ROWS + tm, d), BF16)],
        compiler_params=_params(2, 56),
        name="gated_conv_ffn",
    )(x, x, gain.reshape(1, d), wg, wu, cw, cb.reshape(1, f), wd, fin.reshape(1, d))


def _pick(n, pref):
    t = min(pref, n)
    while n % t:
        t //= 2
    return t


def kernel(x, mix_norm, w_in, conv_w, conv_b, conv_ln_g, conv_ln_b, sg_v_norm, sg_w, sg_b,
           merge_norm, w_out, ffn_norm, w_gate, w_up, ffn_conv_w, ffn_conv_b, w_down, final_norm):
    b, s, d = x.shape
    t = b * s
    depth = w_in.shape[0]
    xf = x.reshape(t, d)
    c1 = SB_WIDTH
    c2 = SB_WIDTH + CONV_WIDTH
    for l in range(depth):
        qkv, rest = _proj(xf, mix_norm[l], w_in[l].astype(BF16), tm=_pick(t, 512), tn=1024)
        mg = merge_norm[l]
        y_sb = _attention(qkv.reshape(b, s, -1), mg[:c1], tq=_pick(s, 256), tk=_pick(s, 256))
        y_cv = _conformer_conv(rest.reshape(b, s, -1), conv_w[l], conv_b[l], conv_ln_g[l],
                               conv_ln_b[l], mg[c1:c2])
        b_full = jnp.broadcast_to(sg_b[l][:, :, None], (SG_HEADS, SG_CHUNK, SG_HEAD_DIM))
        y_sg = _spatial_gating(rest, sg_v_norm[l], sg_w[l], b_full, mg[c2:], tm=_pick(s, 512))
        x1 = _out_proj(xf, y_sb.reshape(t, c1), y_cv.reshape(t, CONV_WIDTH), y_sg,
                       w_out[l].astype(BF16), tm=_pick(t, 512), tn=1024)
        xf = _ffn(x1, ffn_norm[l], w_gate[l].astype(BF16), w_up[l].astype(BF16), ffn_conv_w[l],
                  ffn_conv_b[l], w_down[l].astype(BF16), final_norm, seq=s, tm=_pick(s, 512),
                  fc=512, final=(l == depth - 1))
    return xf.reshape(b, s, d)
```

```python
import functools

import jax
import jax.numpy as jnp
from jax import lax
from jax.experimental import pallas as pl
from jax.experimental.pallas import tpu as pltpu

F32 = jnp.float32
BF16 = jnp.bfloat16

EPS = 1e-6
SB_HEAD_DIM = 64
SB_HEADS = 16
SB_WIDTH = SB_HEAD_DIM * SB_HEADS
CONV_GROUP_DIM = 64
CONV_WIDTH = 512
CONV_KERNEL = 31
SG_HEAD_DIM = 128
SG_HEADS = 4
SG_WIDTH = SG_HEAD_DIM * SG_HEADS
SG_CHUNK = 128
STREAM_CHUNK = 64
FFN_CONV_KERNEL = 3

LANES = 128
SUBLANES = 8
BF16_ROWS = 16
V7X_VMEM_BYTES = 64 * 1024 * 1024


def _params(n_grid, vmem_mb):
    return pltpu.CompilerParams(
        dimension_semantics=("arbitrary",) * n_grid,
        vmem_limit_bytes=min(vmem_mb * 1024 * 1024, V7X_VMEM_BYTES - 4 * 1024 * 1024),
    )


def _rms(x, g):
    ms = jnp.mean(x * x, axis=-1, keepdims=True)
    return x * lax.rsqrt(ms + EPS) * g


def _half_lane_mean(x, low):
    s_low = jnp.sum(jnp.where(low, x, 0.0), axis=-1, keepdims=True)
    s_all = jnp.sum(x, axis=-1, keepdims=True)
    return jnp.where(low, s_low, s_all - s_low) * (1.0 / 64.0)


def _proj_kernel(x_ref, g_ref, w_ref, qkv_ref, rest_ref, h_ref, *, n_qkv_tiles):
    j = pl.program_id(1)

    @pl.when(j == 0)
    def _():
        h_ref[...] = _rms(x_ref[...], g_ref[...]).astype(BF16)

    def mm():
        return jnp.dot(h_ref[...], w_ref[...], preferred_element_type=F32)

    @pl.when(j < n_qkv_tiles)
    def _():
        qkv_ref[...] = mm().astype(BF16)

    @pl.when(j >= n_qkv_tiles)
    def _():
        rest_ref[...] = mm()


def _proj(x, gain, w, *, tm, tn):
    t, d = x.shape
    n_out = w.shape[1]
    n_qkv = 3 * SB_WIDTH
    n_rest = n_out - n_qkv
    nq = n_qkv // tn
    return pl.pallas_call(
        functools.partial(_proj_kernel, n_qkv_tiles=nq),
        grid=(t // tm, n_out // tn),
        in_specs=[
            pl.BlockSpec((tm, d), lambda i, j: (i, 0)),
            pl.BlockSpec((1, d), lambda i, j: (0, 0)),
            pl.BlockSpec((d, tn), lambda i, j: (0, j)),
        ],
        out_specs=[
            pl.BlockSpec((tm, tn), lambda i, j: (i, jnp.minimum(j, nq - 1))),
            pl.BlockSpec((tm, tn), lambda i, j: (i, jnp.maximum(j - nq, 0))),
        ],
        out_shape=[
            jax.ShapeDtypeStruct((t, n_qkv), BF16),
            jax.ShapeDtypeStruct((t, n_rest), F32),
        ],
        scratch_shapes=[pltpu.VMEM((tm, d), BF16)],
        compiler_params=_params(2, 56),
        name="rms_in_proj",
    )(x, gain.reshape(1, d), w)


def _attn_kernel(q_ref, k_ref, v_ref, g_ref, o_ref, acc_ref, car_ref, *, tq, tk, npairs):
    qi = pl.program_id(2)
    n_diag = tq // tk
    nh = 2 * npairs
    low = lax.broadcasted_iota(jnp.int32, (tq, LANES), 1) < SB_HEAD_DIM
    scale = SB_HEAD_DIM ** -0.5
    lanes = [slice(p * LANES, (p + 1) * LANES) for p in range(npairs)]
    q_heads = []
    for p in range(npairs):
        q = q_ref[:, lanes[p]]
        zero = jnp.zeros_like(q)
        q_heads += [jnp.where(low, q, zero) * scale, jnp.where(low, zero, q) * scale]
    tri = (lax.broadcasted_iota(jnp.int32, (tk, tk), 0)
           >= lax.broadcasted_iota(jnp.int32, (tk, tk), 1)).astype(BF16)
    row = lax.broadcasted_iota(jnp.int32, (tq, tk), 0)
    col = lax.broadcasted_iota(jnp.int32, (tq, tk), 1)
    sign = jnp.uint32(0x80000000)

    acc_ref[...] = jnp.zeros_like(acc_ref)
    car_ref[...] = jnp.zeros_like(car_ref)

    def chunk(start, causal):
        kcs = [k_ref[pl.ds(start, tk), lanes[p]] for p in range(npairs)]
        vcs = [v_ref[pl.ds(start, tk), lanes[p]] for p in range(npairs)]
        zs = [lax.dot_general(q_heads[h], kcs[h // 2], (((1,), (1,)), ((), ())),
                              preferred_element_type=F32) for h in range(nh)]
        sps = []
        for z in zs:
            neg_abs = pltpu.bitcast(pltpu.bitcast(z, jnp.uint32) | sign, F32)
            sp = jnp.maximum(z, 0.0) + jnp.log(1.0 + jnp.exp(neg_abs))
            if causal is not None:
                sp = jnp.where(causal, sp, 0.0)
            sps.append(sp)
        withins = [jnp.dot(sp.astype(BF16), tri, preferred_element_type=F32) for sp in sps]
        ws = []
        for h in range(nh):
            car = car_ref[h]
            total = jnp.concatenate([car] * (tk // LANES), axis=-1) + withins[h]
            w = jnp.exp(zs[h] - total)
            if causal is not None:
                w = jnp.where(causal, w, 0.0)
            ws.append(w.astype(BF16))
            car_ref[h] = car + jnp.sum(sps[h], axis=-1, keepdims=True)
        for h in range(nh):
            acc_ref[h] += jnp.dot(ws[h], vcs[h // 2], preferred_element_type=F32)

    for d in range(n_diag - 1, -1, -1):
        chunk(pl.multiple_of(qi * tq + d * tk, tk), (d * tk + col) < row)

    n_off = qi * n_diag

    def body(it, carry):
        chunk(pl.multiple_of((n_off - 1 - it) * tk, tk), None)
        return carry

    lax.fori_loop(0, n_off, body, 0)

    for p in range(npairs):
        y = jnp.where(low, acc_ref[2 * p], acc_ref[2 * p + 1])
        ms = _half_lane_mean(y * y, low)
        o_ref[:, lanes[p]] = (y * lax.rsqrt(ms + EPS) * g_ref[:, lanes[p]]).astype(o_ref.dtype)


def _attention(qkv, gain, *, tq, tk, npairs):
    b, s, _ = qkv.shape
    width = LANES * npairs
    n_groups = SB_WIDTH // width
    return pl.pallas_call(
        functools.partial(_attn_kernel, tq=tq, tk=tk, npairs=npairs),
        grid=(b, n_groups, s // tq),
        in_specs=[
            pl.BlockSpec((None, tq, width), lambda bi, hg, qi: (bi, qi, hg)),
            pl.BlockSpec((None, s, width), lambda bi, hg, qi: (bi, 0, n_groups + hg)),
            pl.BlockSpec((None, s, width), lambda bi, hg, qi: (bi, 0, 2 * n_groups + hg)),
            pl.BlockSpec((1, width), lambda bi, hg, qi: (0, hg)),
        ],
        out_specs=pl.BlockSpec((None, tq, width), lambda bi, hg, qi: (bi, qi, hg)),
        out_shape=jax.ShapeDtypeStruct((b, s, SB_WIDTH), BF16),
        scratch_shapes=[
            pltpu.VMEM((2 * npairs, tq, LANES), F32),
            pltpu.VMEM((2 * npairs, tq, LANES), F32),
        ],
        compiler_params=_params(3, 40),
        name="stick_breaking_attn",
    )(qkv, qkv, qkv, gain.reshape(1, -1))


def _conv_kernel(a_ref, gate_ref, cw_ref, cb_ref, lg_ref, lb_ref, mg_ref, o_ref, hpad_ref,
                 *, s, tr, pad):
    low = lax.broadcasted_iota(jnp.int32, (tr, LANES), 1) < CONV_GROUP_DIM
    hpad_ref[0:pad, :] = jnp.zeros((pad, LANES), F32)

    def glu(it, carry):
        r0 = pl.multiple_of(it * tr, tr)
        a = a_ref[pl.ds(r0, tr), :]
        gt = gate_ref[pl.ds(r0, tr), :]
        hpad_ref[pl.ds(pad + r0, tr), :] = a * jax.nn.sigmoid(gt)
        return carry

    lax.fori_loop(0, s // tr, glu, 0)

    cb = cb_ref[...]
    lg = lg_ref[...]
    lb = lb_ref[...]
    mg = mg_ref[...]
    win = tr + pad

    def tile(it, carry):
        r0 = pl.multiple_of(it * tr, tr)
        x = hpad_ref[pl.ds(r0, win), :]
        acc = jnp.broadcast_to(cb, (tr, LANES))
        for r in range(SUBLANES):
            xr = x if r == 0 else pltpu.roll(x, shift=win - r, axis=0)
            for a8 in range(0, win, SUBLANES):
                k = a8 + r - (pad - (CONV_KERNEL - 1))
                if 0 <= k < CONV_KERNEL and a8 + tr <= win:
                    acc = acc + xr[a8:a8 + tr, :] * cw_ref[pl.ds(k, 1), :]
        mu = _half_lane_mean(acc, low)
        dlt = acc - mu
        var = _half_lane_mean(dlt * dlt, low)
        y = dlt * lax.rsqrt(var + EPS) * lg + lb
        y = y * jax.nn.sigmoid(y)
        ms = _half_lane_mean(y * y, low)
        o_ref[pl.ds(r0, tr), :] = (y * lax.rsqrt(ms + EPS) * mg).astype(o_ref.dtype)
        return carry

    lax.fori_loop(0, s // tr, tile, 0)


def _conformer_conv(rest, cw, cb, lg, lb, mg, *, tr=128, pad=32):
    b, s, _ = rest.shape
    nblk = CONV_WIDTH // LANES
    vec = lambda: pl.BlockSpec((1, LANES), lambda bi, c: (0, c))
    return pl.pallas_call(
        functools.partial(_conv_kernel, s=s, tr=tr, pad=pad),
        grid=(b, nblk),
        in_specs=[
            pl.BlockSpec((None, s, LANES), lambda bi, c: (bi, 0, c)),
            pl.BlockSpec((None, s, LANES), lambda bi, c: (bi, 0, nblk + c)),
            pl.BlockSpec((CONV_KERNEL, LANES), lambda bi, c: (0, c)),
            vec(), vec(), vec(), vec(),
        ],
        out_specs=pl.BlockSpec((None, s, LANES), lambda bi, c: (bi, 0, c)),
        out_shape=jax.ShapeDtypeStruct((b, s, CONV_WIDTH), BF16),
        scratch_shapes=[pltpu.VMEM((s + pad, LANES), F32)],
        compiler_params=_params(2, 32),
        name="conformer_conv",
    )(rest, rest, cw, cb.reshape(1, -1), lg.reshape(1, -1), lb.reshape(1, -1), mg.reshape(1, -1))


def _sg_kernel(u_ref, v_ref, vg_ref, w_ref, b_ref, mg_ref, o_ref, *, tm):
    u = jax.nn.gelu(u_ref[...])
    v = jax.nn.gelu(v_ref[...])
    vg = vg_ref[...]
    mg = mg_ref[...]
    ri = lax.broadcasted_iota(jnp.int32, (SG_CHUNK, SG_CHUNK), 0) // STREAM_CHUNK
    ci = lax.broadcasted_iota(jnp.int32, (SG_CHUNK, SG_CHUNK), 1) // STREAM_CHUNK
    allowed = ci <= ri
    for h in range(SG_HEADS):
        sl = slice(h * SG_HEAD_DIM, (h + 1) * SG_HEAD_DIM)
        vn = _rms(v[:, sl], vg[:, sl]).astype(BF16)
        wh = jnp.where(allowed, w_ref[h], 0.0).astype(BF16)
        for n in range(tm // SG_CHUNK):
            rs = slice(n * SG_CHUNK, (n + 1) * SG_CHUNK)
            mixed = jnp.dot(wh, vn[rs], preferred_element_type=F32) + b_ref[h]
            o_ref[rs, sl] = _rms(u[rs, sl] * mixed, mg[:, sl]).astype(o_ref.dtype)


def _spatial_gating(rest, vg, w, b_full, mg, *, tm):
    t, n_rest = rest.shape
    u_blk = (n_rest - 2 * SG_WIDTH) // SG_WIDTH
    return pl.pallas_call(
        functools.partial(_sg_kernel, tm=tm),
        grid=(t // tm,),
        in_specs=[
            pl.BlockSpec((tm, SG_WIDTH), lambda i: (i, u_blk)),
            pl.BlockSpec((tm, SG_WIDTH), lambda i: (i, u_blk + 1)),
            pl.BlockSpec((1, SG_WIDTH), lambda i: (0, 0)),
            pl.BlockSpec((SG_HEADS, SG_CHUNK, SG_CHUNK), lambda i: (0, 0, 0)),
            pl.BlockSpec((SG_HEADS, SG_CHUNK, SG_HEAD_DIM), lambda i: (0, 0, 0)),
            pl.BlockSpec((1, SG_WIDTH), lambda i: (0, 0)),
        ],
        out_specs=pl.BlockSpec((tm, SG_WIDTH), lambda i: (i, 0)),
        out_shape=jax.ShapeDtypeStruct((t, SG_WIDTH), BF16),
        compiler_params=_params(1, 32),
        name="spatial_gating",
    )(rest, rest, vg.reshape(1, -1), w, b_full, mg.reshape(1, -1))


def _outproj_kernel(x_ref, ysb_ref, ycv_ref, ysg_ref, w_ref, o_ref):
    c1 = SB_WIDTH
    c2 = SB_WIDTH + CONV_WIDTH
    acc = jnp.dot(ysb_ref[...], w_ref[0:c1, :], preferred_element_type=F32)
    acc += jnp.dot(ycv_ref[...], w_ref[c1:c2, :], preferred_element_type=F32)
    acc += jnp.dot(ysg_ref[...], w_ref[c2:, :], preferred_element_type=F32)
    o_ref[...] = x_ref[...] + acc


def _out_proj(x, y_sb, y_cv, y_sg, w, *, tm, tn):
    t, d = x.shape
    k = w.shape[0]
    return pl.pallas_call(
        _outproj_kernel,
        grid=(t // tm, d // tn),
        in_specs=[
            pl.BlockSpec((tm, tn), lambda i, j: (i, j)),
            pl.BlockSpec((tm, SB_WIDTH), lambda i, j: (i, 0)),
            pl.BlockSpec((tm, CONV_WIDTH), lambda i, j: (i, 0)),
            pl.BlockSpec((tm, SG_WIDTH), lambda i, j: (i, 0)),
            pl.BlockSpec((k, tn), lambda i, j: (0, j)),
        ],
        out_specs=pl.BlockSpec((tm, tn), lambda i, j: (i, j)),
        out_shape=jax.ShapeDtypeStruct((t, d), F32),
        compiler_params=_params(2, 56),
        name="out_proj_residual",
    )(x, y_sb, y_cv, y_sg, w)


def _ffn_kernel(x_ref, halo_ref, g_ref, wg_ref, wu_ref, cw_ref, cb_ref, wd_ref, fin_ref,
                o_ref, h_ref, *, tm, tiles_per_seq, final):
    i = pl.program_id(0)
    j = pl.program_id(1)
    hr = BF16_ROWS

    @pl.when(j == 0)
    def _():
        x = x_ref[...]
        g = g_ref[...]
        h_ref[hr:, :] = _rms(x, g).astype(BF16)
        first = (i % tiles_per_seq) == 0
        hh = jnp.where(first, 0.0, _rms(halo_ref[...], g))
        h_ref[0:hr, :] = jnp.concatenate([jnp.zeros_like(hh), hh], axis=0).astype(BF16)
        o_ref[...] = x

    gate = jnp.dot(h_ref[...], wg_ref[...], preferred_element_type=F32)
    up = jnp.dot(h_ref[hr:, :], wu_ref[...], preferred_element_type=F32)
    gc = cb_ref[...] + gate[hr:, :] * cw_ref[2:3, :]
    gc = gc + gate[hr - 1:hr - 1 + tm, :] * cw_ref[1:2, :]
    gc = gc + gate[hr - 2:hr - 2 + tm, :] * cw_ref[0:1, :]
    act = (gc * jax.nn.sigmoid(gc) * up).astype(BF16)
    o_ref[...] += jnp.dot(act, wd_ref[...], preferred_element_type=F32)

    if final:
        @pl.when(j == pl.num_programs(1) - 1)
        def _():
            o_ref[...] = _rms(o_ref[...], fin_ref[...])


def _ffn(x, gain, wg, wu, cw, cb, wd, fin, *, seq, tm, fc, final):
    t, d = x.shape
    f = wg.shape[1]
    halo_blocks = tm // SUBLANES
    return pl.pallas_call(
        functools.partial(_ffn_kernel, tm=tm, tiles_per_seq=seq // tm, final=final),
        grid=(t // tm, f // fc),
        in_specs=[
            pl.BlockSpec((tm, d), lambda i, j: (i, 0)),
            pl.BlockSpec((SUBLANES, d), lambda i, j: (jnp.maximum(i * halo_blocks - 1, 0), 0)),
            pl.BlockSpec((1, d), lambda i, j: (0, 0)),
            pl.BlockSpec((d, fc), lambda i, j: (0, j)),
            pl.BlockSpec((d, fc), lambda i, j: (0, j)),
            pl.BlockSpec((FFN_CONV_KERNEL, fc), lambda i, j: (0, j)),
            pl.BlockSpec((1, fc), lambda i, j: (0, j)),
            pl.BlockSpec((fc, d), lambda i, j: (j, 0)),
            pl.BlockSpec((1, d), lambda i, j: (0, 0)),
        ],
        out_specs=pl.BlockSpec((tm, d), lambda i, j: (i, 0)),
        out_shape=jax.ShapeDtypeStruct((t, d), F32),
        scratch_shapes=[pltpu.VMEM((BF16_ROWS + tm, d), BF16)],
        compiler_params=_params(2, 56),
        name="gated_conv_ffn",
    )(x, x, gain.reshape(1, d), wg, wu, cw, cb.reshape(1, f), wd, fin.reshape(1, d))


def _pick(n, pref):
    t = min(pref, n)
    while n % t:
        t //= 2
    return t


def kernel(x, mix_norm, w_in, conv_w, conv_b, conv_ln_g, conv_ln_b, sg_v_norm, sg_w, sg_b,
           merge_norm, w_out, ffn_norm, w_gate, w_up, ffn_conv_w, ffn_conv_b, w_down, final_norm):
    b, s, d = x.shape
    t = b * s
    depth = w_in.shape[0]
    xf = x.reshape(t, d)
    c1 = SB_WIDTH
    c2 = SB_WIDTH + CONV_WIDTH
    for l in range(depth):
        qkv, rest = _proj(xf, mix_norm[l], w_in[l].astype(BF16), tm=_pick(t, 1024), tn=1024)
        mg = merge_norm[l]
        y_sb = _attention(qkv.reshape(b, s, -1), mg[:c1], tq=_pick(s, 256), tk=_pick(s, 256),
                          npairs=4)
        y_cv = _conformer_conv(rest.reshape(b, s, -1), conv_w[l], conv_b[l], conv_ln_g[l],
                               conv_ln_b[l], mg[c1:c2])
        b_full = jnp.broadcast_to(sg_b[l][:, :, None], (SG_HEADS, SG_CHUNK, SG_HEAD_DIM))
        y_sg = _spatial_gating(rest, sg_v_norm[l], sg_w[l], b_full, mg[c2:], tm=_pick(s, 512))
        x1 = _out_proj(xf, y_sb.reshape(t, c1), y_cv.reshape(t, CONV_WIDTH), y_sg,
                       w_out[l].astype(BF16), tm=_pick(t, 512), tn=d)
        xf = _ffn(x1, ffn_norm[l], w_gate[l].astype(BF16), w_up[l].astype(BF16), ffn_conv_w[l],
                  ffn_conv_b[l], w_down[l].astype(BF16), final_norm, seq=s, tm=_pick(s, 512),
                  fc=512, final=(l == depth - 1))
    return xf.reshape(b, s, d)
```

```python
import functools

import jax
import jax.numpy as jnp
from jax import lax
from jax.experimental import pallas as pl
from jax.experimental.pallas import tpu as pltpu

F32 = jnp.float32
BF16 = jnp.bfloat16

EPS = 1e-6
SB_HEAD_DIM = 64
SB_HEADS = 16
SB_WIDTH = SB_HEAD_DIM * SB_HEADS
CONV_GROUP_DIM = 64
CONV_WIDTH = 512
CONV_KERNEL = 31
SG_HEAD_DIM = 128
SG_HEADS = 4
SG_WIDTH = SG_HEAD_DIM * SG_HEADS
SG_CHUNK = 128
STREAM_CHUNK = 64
FFN_CONV_KERNEL = 3

LANES = 128
SUBLANES = 8
BF16_ROWS = 16
V7X_VMEM_BYTES = 64 * 1024 * 1024


def _params(n_grid, vmem_mb):
    return pltpu.CompilerParams(
        dimension_semantics=("arbitrary",) * n_grid,
        vmem_limit_bytes=min(vmem_mb * 1024 * 1024, V7X_VMEM_BYTES - 4 * 1024 * 1024),
    )


def _rms(x, g):
    ms = jnp.mean(x * x, axis=-1, keepdims=True)
    return x * lax.rsqrt(ms + EPS) * g


def _half_lane_mean(x, low):
    s_low = jnp.sum(jnp.where(low, x, 0.0), axis=-1, keepdims=True)
    s_all = jnp.sum(x, axis=-1, keepdims=True)
    return jnp.where(low, s_low, s_all - s_low) * (1.0 / 64.0)


def _proj_kernel(x_ref, g_ref, w_ref, qkv_ref, rest_ref, h_ref, *, n_qkv_tiles):
    j = pl.program_id(1)

    @pl.when(j == 0)
    def _():
        h_ref[...] = _rms(x_ref[...], g_ref[...]).astype(BF16)

    def mm():
        return jnp.dot(h_ref[...], w_ref[...], preferred_element_type=F32)

    @pl.when(j < n_qkv_tiles)
    def _():
        qkv_ref[...] = mm().astype(BF16)

    @pl.when(j >= n_qkv_tiles)
    def _():
        rest_ref[...] = mm()


def _proj(x, gain, w, layer, *, tm, tn):
    t, d = x.shape
    n_out = w.shape[2]
    n_qkv = 3 * SB_WIDTH
    n_rest = n_out - n_qkv
    nq = n_qkv // tn
    return pl.pallas_call(
        functools.partial(_proj_kernel, n_qkv_tiles=nq),
        grid=(t // tm, n_out // tn),
        in_specs=[
            pl.BlockSpec((tm, d), lambda i, j: (i, 0)),
            pl.BlockSpec((1, d), lambda i, j: (0, 0)),
            pl.BlockSpec((None, d, tn), lambda i, j: (layer, 0, j)),
        ],
        out_specs=[
            pl.BlockSpec((tm, tn), lambda i, j: (i, jnp.minimum(j, nq - 1))),
            pl.BlockSpec((tm, tn), lambda i, j: (i, jnp.maximum(j - nq, 0))),
        ],
        out_shape=[
            jax.ShapeDtypeStruct((t, n_qkv), BF16),
            jax.ShapeDtypeStruct((t, n_rest), F32),
        ],
        scratch_shapes=[pltpu.VMEM((tm, d), BF16)],
        compiler_params=_params(2, 56),
        name="rms_in_proj",
    )(x, gain.reshape(1, d), w)


def _attn_kernel(q_ref, k_ref, v_ref, g_ref, o_ref, acc_ref, car_ref, *, tq, tk, npairs):
    qi = pl.program_id(2)
    n_diag = tq // tk
    nh = 2 * npairs
    low = lax.broadcasted_iota(jnp.int32, (tq, LANES), 1) < SB_HEAD_DIM
    scale = SB_HEAD_DIM ** -0.5
    lanes = [slice(p * LANES, (p + 1) * LANES) for p in range(npairs)]
    q_heads = []
    for p in range(npairs):
        q = q_ref[:, lanes[p]]
        zero = jnp.zeros_like(q)
        q_heads += [jnp.where(low, q, zero) * scale, jnp.where(low, zero, q) * scale]
    tri = (lax.broadcasted_iota(jnp.int32, (tk, tk), 0)
           >= lax.broadcasted_iota(jnp.int32, (tk, tk), 1)).astype(BF16)
    row = lax.broadcasted_iota(jnp.int32, (tq, tk), 0)
    col = lax.broadcasted_iota(jnp.int32, (tq, tk), 1)
    sign = jnp.uint32(0x80000000)

    acc_ref[...] = jnp.zeros_like(acc_ref)
    car_ref[...] = jnp.zeros_like(car_ref)

    def chunk(start, causal):
        kcs = [k_ref[pl.ds(start, tk), lanes[p]] for p in range(npairs)]
        vcs = [v_ref[pl.ds(start, tk), lanes[p]] for p in range(npairs)]
        zs = [lax.dot_general(q_heads[h], kcs[h // 2], (((1,), (1,)), ((), ())),
                              preferred_element_type=F32) for h in range(nh)]
        sps = []
        for z in zs:
            neg_abs = pltpu.bitcast(pltpu.bitcast(z, jnp.uint32) | sign, F32)
            sp = jnp.maximum(z, 0.0) + jnp.log(1.0 + jnp.exp(neg_abs))
            if causal is not None:
                sp = jnp.where(causal, sp, 0.0)
            sps.append(sp)
        withins = [jnp.dot(sp.astype(BF16), tri, preferred_element_type=F32) for sp in sps]
        ws = []
        for h in range(nh):
            car = car_ref[h]
            total = jnp.concatenate([car] * (tk // LANES), axis=-1) + withins[h]
            w = jnp.exp(zs[h] - total)
            if causal is not None:
                w = jnp.where(causal, w, 0.0)
            ws.append(w.astype(BF16))
            car_ref[h] = car + jnp.sum(sps[h], axis=-1, keepdims=True)
        for h in range(nh):
            acc_ref[h] += jnp.dot(ws[h], vcs[h // 2], preferred_element_type=F32)

    for d in range(n_diag - 1, -1, -1):
        chunk(pl.multiple_of(qi * tq + d * tk, tk), (d * tk + col) < row)

    n_off = qi * n_diag

    def body(it, carry):
        chunk(pl.multiple_of((n_off - 1 - it) * tk, tk), None)
        return carry

    lax.fori_loop(0, n_off, body, 0)

    for p in range(npairs):
        y = jnp.where(low, acc_ref[2 * p], acc_ref[2 * p + 1])
        ms = _half_lane_mean(y * y, low)
        o_ref[:, lanes[p]] = (y * lax.rsqrt(ms + EPS) * g_ref[:, lanes[p]]).astype(o_ref.dtype)


def _attention(qkv, gain, *, tq, tk, npairs):
    b, s, _ = qkv.shape
    width = LANES * npairs
    n_groups = SB_WIDTH // width
    return pl.pallas_call(
        functools.partial(_attn_kernel, tq=tq, tk=tk, npairs=npairs),
        grid=(b, n_groups, s // tq),
        in_specs=[
            pl.BlockSpec((None, tq, width), lambda bi, hg, qi: (bi, qi, hg)),
            pl.BlockSpec((None, s, width), lambda bi, hg, qi: (bi, 0, n_groups + hg)),
            pl.BlockSpec((None, s, width), lambda bi, hg, qi: (bi, 0, 2 * n_groups + hg)),
            pl.BlockSpec((1, width), lambda bi, hg, qi: (0, hg)),
        ],
        out_specs=pl.BlockSpec((None, tq, width), lambda bi, hg, qi: (bi, qi, hg)),
        out_shape=jax.ShapeDtypeStruct((b, s, SB_WIDTH), BF16),
        scratch_shapes=[
            pltpu.VMEM((2 * npairs, tq, LANES), F32),
            pltpu.VMEM((2 * npairs, tq, LANES), F32),
        ],
        compiler_params=_params(3, 40),
        name="stick_breaking_attn",
    )(qkv, qkv, qkv, gain.reshape(1, -1))


def _conv_kernel(a_ref, gate_ref, cw_ref, cb_ref, lg_ref, lb_ref, mg_ref, o_ref, hpad_ref,
                 *, s, tr, pad):
    low = lax.broadcasted_iota(jnp.int32, (tr, LANES), 1) < CONV_GROUP_DIM
    hpad_ref[0:pad, :] = jnp.zeros((pad, LANES), F32)

    def glu(it, carry):
        r0 = pl.multiple_of(it * tr, tr)
        a = a_ref[pl.ds(r0, tr), :]
        gt = gate_ref[pl.ds(r0, tr), :]
        hpad_ref[pl.ds(pad + r0, tr), :] = a * jax.nn.sigmoid(gt)
        return carry

    lax.fori_loop(0, s // tr, glu, 0)

    cb = cb_ref[...]
    lg = lg_ref[...]
    lb = lb_ref[...]
    mg = mg_ref[...]

    def tile(it, carry):
        r0 = pl.multiple_of(it * tr, tr)
        acc = jnp.broadcast_to(cb, (tr, LANES))
        for k in range(CONV_KERNEL):
            off = pad - (CONV_KERNEL - 1) + k
            acc = acc + hpad_ref[pl.ds(r0 + off, tr), :] * cw_ref[pl.ds(k, 1), :]
        mu = _half_lane_mean(acc, low)
        dlt = acc - mu
        var = _half_lane_mean(dlt * dlt, low)
        y = dlt * lax.rsqrt(var + EPS) * lg + lb
        y = y * jax.nn.sigmoid(y)
        ms = _half_lane_mean(y * y, low)
        o_ref[pl.ds(r0, tr), :] = (y * lax.rsqrt(ms + EPS) * mg).astype(o_ref.dtype)
        return carry

    lax.fori_loop(0, s // tr, tile, 0, unroll=2)


def _conformer_conv(rest, cw, cb, lg, lb, mg, *, tr=128, pad=32):
    b, s, _ = rest.shape
    nblk = CONV_WIDTH // LANES
    vec = lambda: pl.BlockSpec((1, LANES), lambda bi, c: (0, c))
    return pl.pallas_call(
        functools.partial(_conv_kernel, s=s, tr=tr, pad=pad),
        grid=(b, nblk),
        in_specs=[
            pl.BlockSpec((None, s, LANES), lambda bi, c: (bi, 0, c)),
            pl.BlockSpec((None, s, LANES), lambda bi, c: (bi, 0, nblk + c)),
            pl.BlockSpec((CONV_KERNEL, LANES), lambda bi, c: (0, c)),
            vec(), vec(), vec(), vec(),
        ],
        out_specs=pl.BlockSpec((None, s, LANES), lambda bi, c: (bi, 0, c)),
        out_shape=jax.ShapeDtypeStruct((b, s, CONV_WIDTH), BF16),
        scratch_shapes=[pltpu.VMEM((s + pad, LANES), F32)],
        compiler_params=_params(2, 32),
        name="conformer_conv",
    )(rest, rest, cw, cb.reshape(1, -1), lg.reshape(1, -1), lb.reshape(1, -1), mg.reshape(1, -1))


def _sg_kernel(u_ref, v_ref, vg_ref, w_ref, b_ref, mg_ref, o_ref, *, tm):
    u = jax.nn.gelu(u_ref[...])
    v = jax.nn.gelu(v_ref[...])
    vg = vg_ref[...]
    mg = mg_ref[...]
    ri = lax.broadcasted_iota(jnp.int32, (SG_CHUNK, SG_CHUNK), 0) // STREAM_CHUNK
    ci = lax.broadcasted_iota(jnp.int32, (SG_CHUNK, SG_CHUNK), 1) // STREAM_CHUNK
    allowed = ci <= ri
    for h in range(SG_HEADS):
        sl = slice(h * SG_HEAD_DIM, (h + 1) * SG_HEAD_DIM)
        vn = _rms(v[:, sl], vg[:, sl]).astype(BF16)
        wh = jnp.where(allowed, w_ref[h], 0.0).astype(BF16)
        for n in range(tm // SG_CHUNK):
            rs = slice(n * SG_CHUNK, (n + 1) * SG_CHUNK)
            mixed = jnp.dot(wh, vn[rs], preferred_element_type=F32) + b_ref[h]
            o_ref[rs, sl] = _rms(u[rs, sl] * mixed, mg[:, sl]).astype(o_ref.dtype)


def _spatial_gating(rest, vg, w, b_full, mg, *, tm):
    t, n_rest = rest.shape
    u_blk = (n_rest - 2 * SG_WIDTH) // SG_WIDTH
    return pl.pallas_call(
        functools.partial(_sg_kernel, tm=tm),
        grid=(t // tm,),
        in_specs=[
            pl.BlockSpec((tm, SG_WIDTH), lambda i: (i, u_blk)),
            pl.BlockSpec((tm, SG_WIDTH), lambda i: (i, u_blk + 1)),
            pl.BlockSpec((1, SG_WIDTH), lambda i: (0, 0)),
            pl.BlockSpec((SG_HEADS, SG_CHUNK, SG_CHUNK), lambda i: (0, 0, 0)),
            pl.BlockSpec((SG_HEADS, SG_CHUNK, SG_HEAD_DIM), lambda i: (0, 0, 0)),
            pl.BlockSpec((1, SG_WIDTH), lambda i: (0, 0)),
        ],
        out_specs=pl.BlockSpec((tm, SG_WIDTH), lambda i: (i, 0)),
        out_shape=jax.ShapeDtypeStruct((t, SG_WIDTH), BF16),
        compiler_params=_params(1, 32),
        name="spatial_gating",
    )(rest, rest, vg.reshape(1, -1), w, b_full, mg.reshape(1, -1))


def _outproj_kernel(x_ref, ysb_ref, ycv_ref, ysg_ref, w_ref, o_ref):
    c1 = SB_WIDTH
    c2 = SB_WIDTH + CONV_WIDTH
    acc = jnp.dot(ysb_ref[...], w_ref[0:c1, :], preferred_element_type=F32)
    acc += jnp.dot(ycv_ref[...], w_ref[c1:c2, :], preferred_element_type=F32)
    acc += jnp.dot(ysg_ref[...], w_ref[c2:, :], preferred_element_type=F32)
    o_ref[...] = x_ref[...] + acc


def _out_proj(x, y_sb, y_cv, y_sg, w, layer, *, tm, tn):
    t, d = x.shape
    k = w.shape[1]
    return pl.pallas_call(
        _outproj_kernel,
        grid=(t // tm, d // tn),
        in_specs=[
            pl.BlockSpec((tm, tn), lambda i, j: (i, j)),
            pl.BlockSpec((tm, SB_WIDTH), lambda i, j: (i, 0)),
            pl.BlockSpec((tm, CONV_WIDTH), lambda i, j: (i, 0)),
            pl.BlockSpec((tm, SG_WIDTH), lambda i, j: (i, 0)),
            pl.BlockSpec((None, k, tn), lambda i, j: (layer, 0, j)),
        ],
        out_specs=pl.BlockSpec((tm, tn), lambda i, j: (i, j)),
        out_shape=jax.ShapeDtypeStruct((t, d), F32),
        compiler_params=_params(2, 56),
        name="out_proj_residual",
    )(x, y_sb, y_cv, y_sg, w)


def _ffn_kernel(x_ref, halo_ref, g_ref, wg_ref, wu_ref, cw_ref, cb_ref, wd_ref, fin_ref,
                o_ref, h_ref, *, tm, tiles_per_seq, final):
    i = pl.program_id(0)
    j = pl.program_id(1)
    hr = BF16_ROWS

    @pl.when(j == 0)
    def _():
        x = x_ref[...]
        g = g_ref[...]
        h_ref[hr:, :] = _rms(x, g).astype(BF16)
        first = (i % tiles_per_seq) == 0
        hh = jnp.where(first, 0.0, _rms(halo_ref[...], g))
        h_ref[0:hr, :] = jnp.concatenate([jnp.zeros_like(hh), hh], axis=0).astype(BF16)
        o_ref[...] = x

    gate = jnp.dot(h_ref[...], wg_ref[...], preferred_element_type=F32)
    up = jnp.dot(h_ref[hr:, :], wu_ref[...], preferred_element_type=F32)
    gc = cb_ref[...] + gate[hr:, :] * cw_ref[2:3, :]
    gc = gc + gate[hr - 1:hr - 1 + tm, :] * cw_ref[1:2, :]
    gc = gc + gate[hr - 2:hr - 2 + tm, :] * cw_ref[0:1, :]
    act = (gc * jax.nn.sigmoid(gc) * up).astype(BF16)
    o_ref[...] += jnp.dot(act, wd_ref[...], preferred_element_type=F32)

    if final:
        @pl.when(j == pl.num_programs(1) - 1)
        def _():
            o_ref[...] = _rms(o_ref[...], fin_ref[...])


def _ffn(x, gain, wg, wu, cw, cb, wd, fin, layer, *, seq, tm, fc, final):
    t, d = x.shape
    f = wg.shape[2]
    halo_blocks = tm // SUBLANES
    return pl.pallas_call(
        functools.partial(_ffn_kernel, tm=tm, tiles_per_seq=seq // tm, final=final),
        grid=(t // tm, f // fc),
        in_specs=[
            pl.BlockSpec((tm, d), lambda i, j: (i, 0)),
            pl.BlockSpec((SUBLANES, d), lambda i, j: (jnp.maximum(i * halo_blocks - 1, 0), 0)),
            pl.BlockSpec((1, d), lambda i, j: (0, 0)),
            pl.BlockSpec((None, d, fc), lambda i, j: (layer, 0, j)),
            pl.BlockSpec((None, d, fc), lambda i, j: (layer, 0, j)),
            pl.BlockSpec((FFN_CONV_KERNEL, fc), lambda i, j: (0, j)),
            pl.BlockSpec((1, fc), lambda i, j: (0, j)),
            pl.BlockSpec((None, fc, d), lambda i, j: (layer, j, 0)),
            pl.BlockSpec((1, d), lambda i, j: (0, 0)),
        ],
        out_specs=pl.BlockSpec((tm, d), lambda i, j: (i, 0)),
        out_shape=jax.ShapeDtypeStruct((t, d), F32),
        scratch_shapes=[pltpu.VMEM((BF16_ROWS + tm, d), BF16)],
        compiler_params=_params(2, 56),
        name="gated_conv_ffn",
    )(x, x, gain.reshape(1, d), wg, wu, cw, cb.reshape(1, f), wd, fin.reshape(1, d))


def _pick(n, pref):
    t = min(pref, n)
    while n % t:
        t //= 2
    return t


def _cast_kernel(w_ref, o_ref):
    o_ref[...] = w_ref[...].astype(o_ref.dtype)


def _to_bf16(w, *, tr=512):
    depth, k, n = w.shape
    rows = depth * k
    tr = _pick(rows, tr)
    out = pl.pallas_call(
        _cast_kernel,
        grid=(rows // tr,),
        in_specs=[pl.BlockSpec((tr, n), lambda i: (i, 0))],
        out_specs=pl.BlockSpec((tr, n), lambda i: (i, 0)),
        out_shape=jax.ShapeDtypeStruct((rows, n), BF16),
        compiler_params=_params(1, 48),
        name="weights_to_bf16",
    )(w.reshape(rows, n))
    return out.reshape(depth, k, n)


def kernel(x, mix_norm, w_in, conv_w, conv_b, conv_ln_g, conv_ln_b, sg_v_norm, sg_w, sg_b,
           merge_norm, w_out, ffn_norm, w_gate, w_up, ffn_conv_w, ffn_conv_b, w_down, final_norm):
    b, s, d = x.shape
    t = b * s
    depth = w_in.shape[0]
    xf = x.reshape(t, d)
    c1 = SB_WIDTH
    c2 = SB_WIDTH + CONV_WIDTH
    w_in_b, w_out_b, w_gate_b, w_up_b, w_down_b = (
        _to_bf16(w) for w in (w_in, w_out, w_gate, w_up, w_down))
    for l in range(depth):
        qkv, rest = _proj(xf, mix_norm[l], w_in_b, l, tm=_pick(t, 1024), tn=1024)
        mg = merge_norm[l]
        y_sb = _attention(qkv.reshape(b, s, -1), mg[:c1], tq=_pick(s, 256), tk=_pick(s, 256),
                          npairs=4)
        y_cv = _conformer_conv(rest.reshape(b, s, -1), conv_w[l], conv_b[l], conv_ln_g[l],
                               conv_ln_b[l], mg[c1:c2])
        b_full = jnp.broadcast_to(sg_b[l][:, :, None], (SG_HEADS, SG_CHUNK, SG_HEAD_DIM))
        y_sg = _spatial_gating(rest, sg_v_norm[l], sg_w[l], b_full, mg[c2:], tm=_pick(s, 512))
        x1 = _out_proj(xf, y_sb.reshape(t, c1), y_cv.reshape(t, CONV_WIDTH), y_sg,
                       w_out_b, l, tm=_pick(t, 512), tn=d)
        xf = _ffn(x1, ffn_norm[l], w_gate_b, w_up_b, ffn_conv_w[l], ffn_conv_b[l], w_down_b,
                  final_norm, l, seq=s, tm=_pick(s, 512), fc=512, final=(l == depth - 1))
    return xf.reshape(b, s, d)
```

```python
import functools

import jax
import jax.numpy as jnp
from jax import lax
from jax.experimental import pallas as pl
from jax.experimental.pallas import tpu as pltpu

F32 = jnp.float32
BF16 = jnp.bfloat16

EPS = 1e-6
SB_HEAD_DIM = 64
SB_HEADS = 16
SB_WIDTH = SB_HEAD_DIM * SB_HEADS
CONV_GROUP_DIM = 64
CONV_WIDTH = 512
CONV_KERNEL = 31
SG_HEAD_DIM = 128
SG_HEADS = 4
SG_WIDTH = SG_HEAD_DIM * SG_HEADS
SG_CHUNK = 128
STREAM_CHUNK = 64
FFN_CONV_KERNEL = 3

CARRY_CUTOFF = 128.0

LANES = 128
SUBLANES = 8
BF16_ROWS = 16
V7X_VMEM_BYTES = 64 * 1024 * 1024


def _params(n_grid, vmem_mb):
    return pltpu.CompilerParams(
        dimension_semantics=("arbitrary",) * n_grid,
        vmem_limit_bytes=min(vmem_mb * 1024 * 1024, V7X_VMEM_BYTES - 4 * 1024 * 1024),
    )


def _rms(x, g):
    ms = jnp.mean(x * x, axis=-1, keepdims=True)
    return x * lax.rsqrt(ms + EPS) * g


def _half_lane_mean(x, low):
    s_low = jnp.sum(jnp.where(low, x, 0.0), axis=-1, keepdims=True)
    s_all = jnp.sum(x, axis=-1, keepdims=True)
    return jnp.where(low, s_low, s_all - s_low) * (1.0 / 64.0)


def _proj_kernel(x_ref, g_ref, w_ref, qkv_ref, rest_ref, h_ref, *, n_qkv_tiles):
    j = pl.program_id(1)

    @pl.when(j == 0)
    def _():
        h_ref[...] = _rms(x_ref[...], g_ref[...]).astype(BF16)

    def mm():
        return jnp.dot(h_ref[...], w_ref[...], preferred_element_type=F32)

    @pl.when(j < n_qkv_tiles)
    def _():
        qkv_ref[...] = mm().astype(BF16)

    @pl.when(j >= n_qkv_tiles)
    def _():
        rest_ref[...] = mm()


def _proj(x, gain, w, layer, *, tm, tn):
    t, d = x.shape
    n_out = w.shape[2]
    n_qkv = 3 * SB_WIDTH
    n_rest = n_out - n_qkv
    nq = n_qkv // tn
    return pl.pallas_call(
        functools.partial(_proj_kernel, n_qkv_tiles=nq),
        grid=(t // tm, n_out // tn),
        in_specs=[
            pl.BlockSpec((tm, d), lambda i, j: (i, 0)),
            pl.BlockSpec((1, d), lambda i, j: (0, 0)),
            pl.BlockSpec((None, d, tn), lambda i, j: (layer, 0, j)),
        ],
        out_specs=[
            pl.BlockSpec((tm, tn), lambda i, j: (i, jnp.minimum(j, nq - 1))),
            pl.BlockSpec((tm, tn), lambda i, j: (i, jnp.maximum(j - nq, 0))),
        ],
        out_shape=[
            jax.ShapeDtypeStruct((t, n_qkv), BF16),
            jax.ShapeDtypeStruct((t, n_rest), F32),
        ],
        scratch_shapes=[pltpu.VMEM((tm, d), BF16)],
        compiler_params=_params(2, 56),
        name="rms_in_proj",
    )(x, gain.reshape(1, d), w)


def _attn_kernel(q_ref, k_ref, v_ref, g_ref, o_ref, acc_ref, car_ref, *, tq, tk, npairs):
    qi = pl.program_id(2)
    n_diag = tq // tk
    nh = 2 * npairs
    low = lax.broadcasted_iota(jnp.int32, (tq, LANES), 1) < SB_HEAD_DIM
    scale = SB_HEAD_DIM ** -0.5
    lanes = [slice(p * LANES, (p + 1) * LANES) for p in range(npairs)]
    q_heads = []
    for p in range(npairs):
        q = q_ref[:, lanes[p]]
        zero = jnp.zeros_like(q)
        q_heads += [jnp.where(low, q, zero) * scale, jnp.where(low, zero, q) * scale]
    tri = (lax.broadcasted_iota(jnp.int32, (tk, tk), 0)
           >= lax.broadcasted_iota(jnp.int32, (tk, tk), 1)).astype(BF16)
    row = lax.broadcasted_iota(jnp.int32, (tq, tk), 0)
    col = lax.broadcasted_iota(jnp.int32, (tq, tk), 1)
    sign = jnp.uint32(0x80000000)

    acc_ref[...] = jnp.zeros_like(acc_ref)
    car_ref[...] = jnp.zeros_like(car_ref)

    def chunk(start, causal):
        kcs = [k_ref[pl.ds(start, tk), lanes[p]] for p in range(npairs)]
        vcs = [v_ref[pl.ds(start, tk), lanes[p]] for p in range(npairs)]
        zs = [lax.dot_general(q_heads[h], kcs[h // 2], (((1,), (1,)), ((), ())),
                              preferred_element_type=F32) for h in range(nh)]
        sps = []
        for z in zs:
            neg_abs = pltpu.bitcast(pltpu.bitcast(z, jnp.uint32) | sign, F32)
            sp = jnp.maximum(z, 0.0) + jnp.log(1.0 + jnp.exp(neg_abs))
            if causal is not None:
                sp = jnp.where(causal, sp, 0.0)
            sps.append(sp)
        withins = [jnp.dot(sp.astype(BF16), tri, preferred_element_type=F32) for sp in sps]
        ws = []
        for h in range(nh):
            car = car_ref[h]
            total = jnp.concatenate([car] * (tk // LANES), axis=-1) + withins[h]
            w = jnp.exp(zs[h] - total)
            if causal is not None:
                w = jnp.where(causal, w, 0.0)
            ws.append(w.astype(BF16))
            car_ref[h] = car + jnp.sum(sps[h], axis=-1, keepdims=True)
        for h in range(nh):
            acc_ref[h] += jnp.dot(ws[h], vcs[h // 2], preferred_element_type=F32)

    for d in range(n_diag - 1, -1, -1):
        chunk(pl.multiple_of(qi * tq + d * tk, tk), (d * tk + col) < row)

    n_off = qi * n_diag

    def more(state):
        it, smallest = state
        return jnp.logical_and(it < n_off, smallest <= CARRY_CUTOFF)

    def body(state):
        it, _ = state
        chunk(pl.multiple_of((n_off - 1 - it) * tk, tk), None)
        car = car_ref[0]
        for h in range(1, nh):
            car = jnp.minimum(car, car_ref[h])
        smallest = jnp.min(jnp.min(car, axis=0, keepdims=True), axis=1, keepdims=True)
        return it + 1, smallest[0, 0]

    lax.while_loop(more, body, (jnp.int32(0), jnp.float32(0.0)))

    for p in range(npairs):
        y = jnp.where(low, acc_ref[2 * p], acc_ref[2 * p + 1])
        ms = _half_lane_mean(y * y, low)
        o_ref[:, lanes[p]] = (y * lax.rsqrt(ms + EPS) * g_ref[:, lanes[p]]).astype(o_ref.dtype)


def _attention(qkv, gain, *, tq, tk, npairs):
    b, s, _ = qkv.shape
    width = LANES * npairs
    n_groups = SB_WIDTH // width
    return pl.pallas_call(
        functools.partial(_attn_kernel, tq=tq, tk=tk, npairs=npairs),
        grid=(b, n_groups, s // tq),
        in_specs=[
            pl.BlockSpec((None, tq, width), lambda bi, hg, qi: (bi, qi, hg)),
            pl.BlockSpec((None, s, width), lambda bi, hg, qi: (bi, 0, n_groups + hg)),
            pl.BlockSpec((None, s, width), lambda bi, hg, qi: (bi, 0, 2 * n_groups + hg)),
            pl.BlockSpec((1, width), lambda bi, hg, qi: (0, hg)),
        ],
        out_specs=pl.BlockSpec((None, tq, width), lambda bi, hg, qi: (bi, qi, hg)),
        out_shape=jax.ShapeDtypeStruct((b, s, SB_WIDTH), BF16),
        scratch_shapes=[
            pltpu.VMEM((2 * npairs, tq, LANES), F32),
            pltpu.VMEM((2 * npairs, tq, LANES), F32),
        ],
        compiler_params=_params(3, 40),
        name="stick_breaking_attn",
    )(qkv, qkv, qkv, gain.reshape(1, -1))


def _conv_kernel(a_ref, gate_ref, cw_ref, cb_ref, lg_ref, lb_ref, mg_ref, o_ref, hpad_ref,
                 *, s, tr, pad):
    low = lax.broadcasted_iota(jnp.int32, (tr, LANES), 1) < CONV_GROUP_DIM
    hpad_ref[0:pad, :] = jnp.zeros((pad, LANES), F32)

    def glu(it, carry):
        r0 = pl.multiple_of(it * tr, tr)
        a = a_ref[pl.ds(r0, tr), :]
        gt = gate_ref[pl.ds(r0, tr), :]
        hpad_ref[pl.ds(pad + r0, tr), :] = a * jax.nn.sigmoid(gt)
        return carry

    lax.fori_loop(0, s // tr, glu, 0)

    cb = cb_ref[...]
    lg = lg_ref[...]
    lb = lb_ref[...]
    mg = mg_ref[...]

    def tile(it, carry):
        r0 = pl.multiple_of(it * tr, tr)
        acc = jnp.broadcast_to(cb, (tr, LANES))
        for k in range(CONV_KERNEL):
            off = pad - (CONV_KERNEL - 1) + k
            acc = acc + hpad_ref[pl.ds(r0 + off, tr), :] * cw_ref[pl.ds(k, 1), :]
        mu = _half_lane_mean(acc, low)
        dlt = acc - mu
        var = _half_lane_mean(dlt * dlt, low)
        y = dlt * lax.rsqrt(var + EPS) * lg + lb
        y = y * jax.nn.sigmoid(y)
        ms = _half_lane_mean(y * y, low)
        o_ref[pl.ds(r0, tr), :] = (y * lax.rsqrt(ms + EPS) * mg).astype(o_ref.dtype)
        return carry

    lax.fori_loop(0, s // tr, tile, 0, unroll=2)


def _conformer_conv(rest, cw, cb, lg, lb, mg, *, tr=128, pad=32):
    b, s, _ = rest.shape
    nblk = CONV_WIDTH // LANES
    vec = lambda: pl.BlockSpec((1, LANES), lambda bi, c: (0, c))
    return pl.pallas_call(
        functools.partial(_conv_kernel, s=s, tr=tr, pad=pad),
        grid=(b, nblk),
        in_specs=[
            pl.BlockSpec((None, s, LANES), lambda bi, c: (bi, 0, c)),
            pl.BlockSpec((None, s, LANES), lambda bi, c: (bi, 0, nblk + c)),
            pl.BlockSpec((CONV_KERNEL, LANES), lambda bi, c: (0, c)),
            vec(), vec(), vec(), vec(),
        ],
        out_specs=pl.BlockSpec((None, s, LANES), lambda bi, c: (bi, 0, c)),
        out_shape=jax.ShapeDtypeStruct((b, s, CONV_WIDTH), BF16),
        scratch_shapes=[pltpu.VMEM((s + pad, LANES), F32)],
        compiler_params=_params(2, 32),
        name="conformer_conv",
    )(rest, rest, cw, cb.reshape(1, -1), lg.reshape(1, -1), lb.reshape(1, -1), mg.reshape(1, -1))


def _sg_kernel(u_ref, v_ref, vg_ref, w_ref, b_ref, mg_ref, o_ref, *, tm):
    u = jax.nn.gelu(u_ref[...])
    v = jax.nn.gelu(v_ref[...])
    vg = vg_ref[...]
    mg = mg_ref[...]
    ri = lax.broadcasted_iota(jnp.int32, (SG_CHUNK, SG_CHUNK), 0) // STREAM_CHUNK
    ci = lax.broadcasted_iota(jnp.int32, (SG_CHUNK, SG_CHUNK), 1) // STREAM_CHUNK
    allowed = ci <= ri
    for h in range(SG_HEADS):
        sl = slice(h * SG_HEAD_DIM, (h + 1) * SG_HEAD_DIM)
        vn = _rms(v[:, sl], vg[:, sl]).astype(BF16)
        wh = jnp.where(allowed, w_ref[h], 0.0).astype(BF16)
        for n in range(tm // SG_CHUNK):
            rs = slice(n * SG_CHUNK, (n + 1) * SG_CHUNK)
            mixed = jnp.dot(wh, vn[rs], preferred_element_type=F32) + b_ref[h]
            o_ref[rs, sl] = _rms(u[rs, sl] * mixed, mg[:, sl]).astype(o_ref.dtype)


def _spatial_gating(rest, vg, w, b_full, mg, *, tm):
    t, n_rest = rest.shape
    u_blk = (n_rest - 2 * SG_WIDTH) // SG_WIDTH
    return pl.pallas_call(
        functools.partial(_sg_kernel, tm=tm),
        grid=(t // tm,),
        in_specs=[
            pl.BlockSpec((tm, SG_WIDTH), lambda i: (i, u_blk)),
            pl.BlockSpec((tm, SG_WIDTH), lambda i: (i, u_blk + 1)),
            pl.BlockSpec((1, SG_WIDTH), lambda i: (0, 0)),
            pl.BlockSpec((SG_HEADS, SG_CHUNK, SG_CHUNK), lambda i: (0, 0, 0)),
            pl.BlockSpec((SG_HEADS, SG_CHUNK, SG_HEAD_DIM), lambda i: (0, 0, 0)),
            pl.BlockSpec((1, SG_WIDTH), lambda i: (0, 0)),
        ],
        out_specs=pl.BlockSpec((tm, SG_WIDTH), lambda i: (i, 0)),
        out_shape=jax.ShapeDtypeStruct((t, SG_WIDTH), BF16),
        compiler_params=_params(1, 32),
        name="spatial_gating",
    )(rest, rest, vg.reshape(1, -1), w, b_full, mg.reshape(1, -1))


def _outproj_kernel(x_ref, ysb_ref, ycv_ref, ysg_ref, w_ref, o_ref):
    c1 = SB_WIDTH
    c2 = SB_WIDTH + CONV_WIDTH
    acc = jnp.dot(ysb_ref[...], w_ref[0:c1, :], preferred_element_type=F32)
    acc += jnp.dot(ycv_ref[...], w_ref[c1:c2, :], preferred_element_type=F32)
    acc += jnp.dot(ysg_ref[...], w_ref[c2:, :], preferred_element_type=F32)
    o_ref[...] = x_ref[...] + acc


def _out_proj(x, y_sb, y_cv, y_sg, w, layer, *, tm, tn):
    t, d = x.shape
    k = w.shape[1]
    return pl.pallas_call(
        _outproj_kernel,
        grid=(t // tm, d // tn),
        in_specs=[
            pl.BlockSpec((tm, tn), lambda i, j: (i, j)),
            pl.BlockSpec((tm, SB_WIDTH), lambda i, j: (i, 0)),
            pl.BlockSpec((tm, CONV_WIDTH), lambda i, j: (i, 0)),
            pl.BlockSpec((tm, SG_WIDTH), lambda i, j: (i, 0)),
            pl.BlockSpec((None, k, tn), lambda i, j: (layer, 0, j)),
        ],
        out_specs=pl.BlockSpec((tm, tn), lambda i, j: (i, j)),
        out_shape=jax.ShapeDtypeStruct((t, d), F32),
        compiler_params=_params(2, 56),
        name="out_proj_residual",
    )(x, y_sb, y_cv, y_sg, w)


def _ffn_kernel(x_ref, halo_ref, g_ref, wg_ref, wu_ref, cw_ref, cb_ref, wd_ref, fin_ref,
                o_ref, h_ref, *, tm, tiles_per_seq, final):
    i = pl.program_id(0)
    j = pl.program_id(1)
    hr = BF16_ROWS

    @pl.when(j == 0)
    def _():
        x = x_ref[...]
        g = g_ref[...]
        h_ref[hr:, :] = _rms(x, g).astype(BF16)
        first = (i % tiles_per_seq) == 0
        hh = jnp.where(first, 0.0, _rms(halo_ref[...], g))
        h_ref[0:hr, :] = jnp.concatenate([jnp.zeros_like(hh), hh], axis=0).astype(BF16)
        o_ref[...] = x

    gate = jnp.dot(h_ref[...], wg_ref[...], preferred_element_type=F32)
    up = jnp.dot(h_ref[hr:, :], wu_ref[...], preferred_element_type=F32)
    gc = cb_ref[...] + gate[hr:, :] * cw_ref[2:3, :]
    gc = gc + gate[hr - 1:hr - 1 + tm, :] * cw_ref[1:2, :]
    gc = gc + gate[hr - 2:hr - 2 + tm, :] * cw_ref[0:1, :]
    act = (gc * jax.nn.sigmoid(gc) * up).astype(BF16)
    o_ref[...] += jnp.dot(act, wd_ref[...], preferred_element_type=F32)

    if final:
        @pl.when(j == pl.num_programs(1) - 1)
        def _():
            o_ref[...] = _rms(o_ref[...], fin_ref[...])


def _ffn(x, gain, wg, wu, cw, cb, wd, fin, layer, *, seq, tm, fc, final):
    t, d = x.shape
    f = wg.shape[2]
    halo_blocks = tm // SUBLANES
    return pl.pallas_call(
        functools.partial(_ffn_kernel, tm=tm, tiles_per_seq=seq // tm, final=final),
        grid=(t // tm, f // fc),
        in_specs=[
            pl.BlockSpec((tm, d), lambda i, j: (i, 0)),
            pl.BlockSpec((SUBLANES, d), lambda i, j: (jnp.maximum(i * halo_blocks - 1, 0), 0)),
            pl.BlockSpec((1, d), lambda i, j: (0, 0)),
            pl.BlockSpec((None, d, fc), lambda i, j: (layer, 0, j)),
            pl.BlockSpec((None, d, fc), lambda i, j: (layer, 0, j)),
            pl.BlockSpec((FFN_CONV_KERNEL, fc), lambda i, j: (0, j)),
            pl.BlockSpec((1, fc), lambda i, j: (0, j)),
            pl.BlockSpec((None, fc, d), lambda i, j: (layer, j, 0)),
            pl.BlockSpec((1, d), lambda i, j: (0, 0)),
        ],
        out_specs=pl.BlockSpec((tm, d), lambda i, j: (i, 0)),
        out_shape=jax.ShapeDtypeStruct((t, d), F32),
        scratch_shapes=[pltpu.VMEM((BF16_ROWS + tm, d), BF16)],
        compiler_params=_params(2, 56),
        name="gated_conv_ffn",
    )(x, x, gain.reshape(1, d), wg, wu, cw, cb.reshape(1, f), wd, fin.reshape(1, d))


def _pick(n, pref):
    t = min(pref, n)
    while n % t:
        t //= 2
    return t


def _cast_kernel(w_ref, o_ref):
    o_ref[...] = w_ref[...].astype(o_ref.dtype)


def _to_bf16(w, *, tr=512):
    depth, k, n = w.shape
    rows = depth * k
    tr = _pick(rows, tr)
    out = pl.pallas_call(
        _cast_kernel,
        grid=(rows // tr,),
        in_specs=[pl.BlockSpec((tr, n), lambda i: (i, 0))],
        out_specs=pl.BlockSpec((tr, n), lambda i: (i, 0)),
        out_shape=jax.ShapeDtypeStruct((rows, n), BF16),
        compiler_params=_params(1, 48),
        name="weights_to_bf16",
    )(w.reshape(rows, n))
    return out.reshape(depth, k, n)


def kernel(x, mix_norm, w_in, conv_w, conv_b, conv_ln_g, conv_ln_b, sg_v_norm, sg_w, sg_b,
           merge_norm, w_out, ffn_norm, w_gate, w_up, ffn_conv_w, ffn_conv_b, w_down, final_norm):
    b, s, d = x.shape
    t = b * s
    depth = w_in.shape[0]
    xf = x.reshape(t, d)
    c1 = SB_WIDTH
    c2 = SB_WIDTH + CONV_WIDTH
    w_in_b, w_out_b, w_gate_b, w_up_b, w_down_b = (
        _to_bf16(w) for w in (w_in, w_out, w_gate, w_up, w_down))
    for l in range(depth):
        qkv, rest = _proj(xf, mix_norm[l], w_in_b, l, tm=_pick(t, 1024), tn=1024)
        mg = merge_norm[l]
        y_sb = _attention(qkv.reshape(b, s, -1), mg[:c1], tq=_pick(s, 256), tk=_pick(s, 256),
                          npairs=4)
        y_cv = _conformer_conv(rest.reshape(b, s, -1), conv_w[l], conv_b[l], conv_ln_g[l],
                               conv_ln_b[l], mg[c1:c2])
        b_full = jnp.broadcast_to(sg_b[l][:, :, None], (SG_HEADS, SG_CHUNK, SG_HEAD_DIM))
        y_sg = _spatial_gating(rest, sg_v_norm[l], sg_w[l], b_full, mg[c2:], tm=_pick(s, 512))
        x1 = _out_proj(xf, y_sb.reshape(t, c1), y_cv.reshape(t, CONV_WIDTH), y_sg,
                       w_out_b, l, tm=_pick(t, 512), tn=d)
        xf = _ffn(x1, ffn_norm[l], w_gate_b, w_up_b, ffn_conv_w[l], ffn_conv_b[l], w_down_b,
                  final_norm, l, seq=s, tm=_pick(s, 1024), fc=512, final=(l == depth - 1))
    return xf.reshape(b, s, d)
```

```python
import functools

import jax
import jax.numpy as jnp
from jax import lax
from jax.experimental import pallas as pl
from jax.experimental.pallas import tpu as pltpu

F32 = jnp.float32
BF16 = jnp.bfloat16

EPS = 1e-6
SB_HEAD_DIM = 64
SB_HEADS = 16
SB_WIDTH = SB_HEAD_DIM * SB_HEADS
CONV_GROUP_DIM = 64
CONV_WIDTH = 512
CONV_KERNEL = 31
SG_HEAD_DIM = 128
SG_HEADS = 4
SG_WIDTH = SG_HEAD_DIM * SG_HEADS
SG_CHUNK = 128
STREAM_CHUNK = 64
FFN_CONV_KERNEL = 3

CARRY_CUTOFF = 128.0

LANES = 128
SUBLANES = 8
BF16_ROWS = 16
V7X_VMEM_BYTES = 64 * 1024 * 1024


def _params(n_grid, vmem_mb):
    return pltpu.CompilerParams(
        dimension_semantics=("arbitrary",) * n_grid,
        vmem_limit_bytes=min(vmem_mb * 1024 * 1024, V7X_VMEM_BYTES - 4 * 1024 * 1024),
    )


def _rms(x, g):
    ms = jnp.mean(x * x, axis=-1, keepdims=True)
    return x * lax.rsqrt(ms + EPS) * g


def _half_lane_mean(x, low):
    s_low = jnp.sum(jnp.where(low, x, 0.0), axis=-1, keepdims=True)
    s_all = jnp.sum(x, axis=-1, keepdims=True)
    return jnp.where(low, s_low, s_all - s_low) * (1.0 / 64.0)


def _proj_kernel(x_ref, g_ref, w_ref, qkv_ref, rest_ref, h_ref, *, n_qkv_tiles):
    j = pl.program_id(1)

    @pl.when(j == 0)
    def _():
        h_ref[...] = _rms(x_ref[...], g_ref[...]).astype(BF16)

    def mm():
        return jnp.dot(h_ref[...], w_ref[...], preferred_element_type=F32)

    @pl.when(j < n_qkv_tiles)
    def _():
        qkv_ref[...] = mm().astype(BF16)

    @pl.when(j >= n_qkv_tiles)
    def _():
        rest_ref[...] = mm()


def _proj(x, gain, w, *, tm, tn):
    t, d = x.shape
    n_out = w.shape[1]
    n_qkv = 3 * SB_WIDTH
    n_rest = n_out - n_qkv
    nq = n_qkv // tn
    return pl.pallas_call(
        functools.partial(_proj_kernel, n_qkv_tiles=nq),
        grid=(t // tm, n_out // tn),
        in_specs=[
            pl.BlockSpec((tm, d), lambda i, j: (i, 0)),
            pl.BlockSpec((1, d), lambda i, j: (0, 0)),
            pl.BlockSpec((d, tn), lambda i, j: (0, j)),
        ],
        out_specs=[
            pl.BlockSpec((tm, tn), lambda i, j: (i, jnp.minimum(j, nq - 1))),
            pl.BlockSpec((tm, tn), lambda i, j: (i, jnp.maximum(j - nq, 0))),
        ],
        out_shape=[
            jax.ShapeDtypeStruct((t, n_qkv), BF16),
            jax.ShapeDtypeStruct((t, n_rest), F32),
        ],
        scratch_shapes=[pltpu.VMEM((tm, d), BF16)],
        compiler_params=_params(2, 56),
        name="rms_in_proj",
    )(x, gain.reshape(1, d), w)


def _attn_kernel(q_ref, k_ref, v_ref, g_ref, *refs, tq, tk, npairs, n_cast):
    cast_in = refs[:n_cast]
    o_ref = refs[n_cast]
    cast_out = refs[n_cast + 1:2 * n_cast + 1]
    acc_ref, car_ref = refs[2 * n_cast + 1:]
    for src, dst in zip(cast_in, cast_out):
        dst[...] = src[...].astype(dst.dtype)

    qi = pl.program_id(2)
    n_diag = tq // tk
    nh = 2 * npairs
    low = lax.broadcasted_iota(jnp.int32, (tq, LANES), 1) < SB_HEAD_DIM
    scale = SB_HEAD_DIM ** -0.5
    lanes = [slice(p * LANES, (p + 1) * LANES) for p in range(npairs)]
    q_heads = []
    for p in range(npairs):
        q = q_ref[:, lanes[p]]
        zero = jnp.zeros_like(q)
        q_heads += [jnp.where(low, q, zero) * scale, jnp.where(low, zero, q) * scale]
    tri = (lax.broadcasted_iota(jnp.int32, (tk, tk), 0)
           >= lax.broadcasted_iota(jnp.int32, (tk, tk), 1)).astype(BF16)
    row = lax.broadcasted_iota(jnp.int32, (tq, tk), 0)
    col = lax.broadcasted_iota(jnp.int32, (tq, tk), 1)
    sign = jnp.uint32(0x80000000)

    acc_ref[...] = jnp.zeros_like(acc_ref)
    car_ref[...] = jnp.zeros_like(car_ref)

    def chunk(start, causal):
        kcs = [k_ref[pl.ds(start, tk), lanes[p]] for p in range(npairs)]
        vcs = [v_ref[pl.ds(start, tk), lanes[p]] for p in range(npairs)]
        zs = [lax.dot_general(q_heads[h], kcs[h // 2], (((1,), (1,)), ((), ())),
                              preferred_element_type=F32) for h in range(nh)]
        sps = []
        for z in zs:
            neg_abs = pltpu.bitcast(pltpu.bitcast(z, jnp.uint32) | sign, F32)
            sp = jnp.maximum(z, 0.0) + jnp.log(1.0 + jnp.exp(neg_abs))
            if causal is not None:
                sp = jnp.where(causal, sp, 0.0)
            sps.append(sp)
        withins = [jnp.dot(sp.astype(BF16), tri, preferred_element_type=F32) for sp in sps]
        ws = []
        for h in range(nh):
            car = car_ref[h]
            total = jnp.concatenate([car] * (tk // LANES), axis=-1) + withins[h]
            w = jnp.exp(zs[h] - total)
            if causal is not None:
                w = jnp.where(causal, w, 0.0)
            ws.append(w.astype(BF16))
            car_ref[h] = car + jnp.sum(sps[h], axis=-1, keepdims=True)
        for h in range(nh):
            acc_ref[h] += jnp.dot(ws[h], vcs[h // 2], preferred_element_type=F32)

    for d in range(n_diag - 1, -1, -1):
        chunk(pl.multiple_of(qi * tq + d * tk, tk), (d * tk + col) < row)

    n_off = qi * n_diag

    def more(state):
        it, smallest = state
        return jnp.logical_and(it < n_off, smallest <= CARRY_CUTOFF)

    def body(state):
        it, _ = state
        chunk(pl.multiple_of((n_off - 1 - it) * tk, tk), None)
        car = car_ref[0]
        for h in range(1, nh):
            car = jnp.minimum(car, car_ref[h])
        smallest = jnp.min(jnp.min(car, axis=0, keepdims=True), axis=1, keepdims=True)
        return it + 1, smallest[0, 0]

    lax.while_loop(more, body, (jnp.int32(0), jnp.float32(0.0)))

    for p in range(npairs):
        y = jnp.where(low, acc_ref[2 * p], acc_ref[2 * p + 1])
        ms = _half_lane_mean(y * y, low)
        o_ref[:, lanes[p]] = (y * lax.rsqrt(ms + EPS) * g_ref[:, lanes[p]]).astype(o_ref.dtype)


def _attention(qkv, gain, cast_jobs, *, tq, tk, npairs):
    b, s, _ = qkv.shape
    width = LANES * npairs
    n_groups = SB_WIDTH // width
    n_q = s // tq
    n_steps = b * n_groups * n_q
    slab_rows = n_steps * BF16_ROWS

    def step(bi, hg, qi):
        return (bi * n_groups + hg) * n_q + qi

    cast_specs_in, cast_specs_out, cast_shapes, cast_args = [], [], [], []
    for w, layer in cast_jobs:
        depth, k, n = w.shape
        cols = (k * n) // slab_rows
        assert cols * slab_rows == k * n and cols % LANES == 0, (k, n, slab_rows)
        cast_args.append(w.reshape(depth, slab_rows, cols))
        cast_specs_in.append(pl.BlockSpec(
            (None, BF16_ROWS, cols), lambda bi, hg, qi, layer=layer: (layer, step(bi, hg, qi), 0)))
        cast_specs_out.append(pl.BlockSpec(
            (BF16_ROWS, cols), lambda bi, hg, qi: (step(bi, hg, qi), 0)))
        cast_shapes.append(jax.ShapeDtypeStruct((slab_rows, cols), BF16))

    outs = pl.pallas_call(
        functools.partial(_attn_kernel, tq=tq, tk=tk, npairs=npairs, n_cast=len(cast_jobs)),
        grid=(b, n_groups, n_q),
        in_specs=[
            pl.BlockSpec((None, tq, width), lambda bi, hg, qi: (bi, qi, hg)),
            pl.BlockSpec((None, s, width), lambda bi, hg, qi: (bi, 0, n_groups + hg)),
            pl.BlockSpec((None, s, width), lambda bi, hg, qi: (bi, 0, 2 * n_groups + hg)),
            pl.BlockSpec((1, width), lambda bi, hg, qi: (0, hg)),
        ] + cast_specs_in,
        out_specs=[pl.BlockSpec((None, tq, width), lambda bi, hg, qi: (bi, qi, hg))]
        + cast_specs_out,
        out_shape=[jax.ShapeDtypeStruct((b, s, SB_WIDTH), BF16)] + cast_shapes,
        scratch_shapes=[
            pltpu.VMEM((2 * npairs, tq, LANES), F32),
            pltpu.VMEM((2 * npairs, tq, LANES), F32),
        ],
        compiler_params=_params(3, 48),
        name="stick_breaking_attn",
    )(qkv, qkv, qkv, gain.reshape(1, -1), *cast_args)
    weights = [o.reshape(w.shape[1], w.shape[2]) for o, (w, _) in zip(outs[1:], cast_jobs)]
    return outs[0], weights


def _conv_kernel(a_ref, gate_ref, cw_ref, cb_ref, lg_ref, lb_ref, mg_ref, o_ref, hpad_ref,
                 *, s, tr, pad):
    low = lax.broadcasted_iota(jnp.int32, (tr, LANES), 1) < CONV_GROUP_DIM
    hpad_ref[0:pad, :] = jnp.zeros((pad, LANES), F32)

    def glu(it, carry):
        r0 = pl.multiple_of(it * tr, tr)
        a = a_ref[pl.ds(r0, tr), :]
        gt = gate_ref[pl.ds(r0, tr), :]
        hpad_ref[pl.ds(pad + r0, tr), :] = a * jax.nn.sigmoid(gt)
        return carry

    lax.fori_loop(0, s // tr, glu, 0)

    cb = cb_ref[...]
    lg = lg_ref[...]
    lb = lb_ref[...]
    mg = mg_ref[...]

    def tile(it, carry):
        r0 = pl.multiple_of(it * tr, tr)
        acc = jnp.broadcast_to(cb, (tr, LANES))
        for k in range(CONV_KERNEL):
            off = pad - (CONV_KERNEL - 1) + k
            acc = acc + hpad_ref[pl.ds(r0 + off, tr), :] * cw_ref[pl.ds(k, 1), :]
        mu = _half_lane_mean(acc, low)
        dlt = acc - mu
        var = _half_lane_mean(dlt * dlt, low)
        y = dlt * lax.rsqrt(var + EPS) * lg + lb
        y = y * jax.nn.sigmoid(y)
        ms = _half_lane_mean(y * y, low)
        o_ref[pl.ds(r0, tr), :] = (y * lax.rsqrt(ms + EPS) * mg).astype(o_ref.dtype)
        return carry

    lax.fori_loop(0, s // tr, tile, 0, unroll=2)


def _conformer_conv(rest, cw, cb, lg, lb, mg, *, tr=128, pad=32):
    b, s, _ = rest.shape
    nblk = CONV_WIDTH // LANES
    vec = lambda: pl.BlockSpec((1, LANES), lambda bi, c: (0, c))
    return pl.pallas_call(
        functools.partial(_conv_kernel, s=s, tr=tr, pad=pad),
        grid=(b, nblk),
        in_specs=[
            pl.BlockSpec((None, s, LANES), lambda bi, c: (bi, 0, c)),
            pl.BlockSpec((None, s, LANES), lambda bi, c: (bi, 0, nblk + c)),
            pl.BlockSpec((CONV_KERNEL, LANES), lambda bi, c: (0, c)),
            vec(), vec(), vec(), vec(),
        ],
        out_specs=pl.BlockSpec((None, s, LANES), lambda bi, c: (bi, 0, c)),
        out_shape=jax.ShapeDtypeStruct((b, s, CONV_WIDTH), BF16),
        scratch_shapes=[pltpu.VMEM((s + pad, LANES), F32)],
        compiler_params=_params(2, 32),
        name="conformer_conv",
    )(rest, rest, cw, cb.reshape(1, -1), lg.reshape(1, -1), lb.reshape(1, -1), mg.reshape(1, -1))


def _sg_kernel(u_ref, v_ref, vg_ref, w_ref, b_ref, mg_ref, o_ref, *, tm):
    u = jax.nn.gelu(u_ref[...])
    v = jax.nn.gelu(v_ref[...])
    vg = vg_ref[...]
    mg = mg_ref[...]
    ri = lax.broadcasted_iota(jnp.int32, (SG_CHUNK, SG_CHUNK), 0) // STREAM_CHUNK
    ci = lax.broadcasted_iota(jnp.int32, (SG_CHUNK, SG_CHUNK), 1) // STREAM_CHUNK
    allowed = ci <= ri
    for h in range(SG_HEADS):
        sl = slice(h * SG_HEAD_DIM, (h + 1) * SG_HEAD_DIM)
        vn = _rms(v[:, sl], vg[:, sl]).astype(BF16)
        wh = jnp.where(allowed, w_ref[h], 0.0).astype(BF16)
        for n in range(tm // SG_CHUNK):
            rs = slice(n * SG_CHUNK, (n + 1) * SG_CHUNK)
            mixed = jnp.dot(wh, vn[rs], preferred_element_type=F32) + b_ref[h]
            o_ref[rs, sl] = _rms(u[rs, sl] * mixed, mg[:, sl]).astype(o_ref.dtype)


def _spatial_gating(rest, vg, w, b_full, mg, *, tm):
    t, n_rest = rest.shape
    u_blk = (n_rest - 2 * SG_WIDTH) // SG_WIDTH
    return pl.pallas_call(
        functools.partial(_sg_kernel, tm=tm),
        grid=(t // tm,),
        in_specs=[
            pl.BlockSpec((tm, SG_WIDTH), lambda i: (i, u_blk)),
            pl.BlockSpec((tm, SG_WIDTH), lambda i: (i, u_blk + 1)),
            pl.BlockSpec((1, SG_WIDTH), lambda i: (0, 0)),
            pl.BlockSpec((SG_HEADS, SG_CHUNK, SG_CHUNK), lambda i: (0, 0, 0)),
            pl.BlockSpec((SG_HEADS, SG_CHUNK, SG_HEAD_DIM), lambda i: (0, 0, 0)),
            pl.BlockSpec((1, SG_WIDTH), lambda i: (0, 0)),
        ],
        out_specs=pl.BlockSpec((tm, SG_WIDTH), lambda i: (i, 0)),
        out_shape=jax.ShapeDtypeStruct((t, SG_WIDTH), BF16),
        compiler_params=_params(1, 32),
        name="spatial_gating",
    )(rest, rest, vg.reshape(1, -1), w, b_full, mg.reshape(1, -1))


def _outproj_kernel(x_ref, ysb_ref, ycv_ref, ysg_ref, w_ref, o_ref):
    c1 = SB_WIDTH
    c2 = SB_WIDTH + CONV_WIDTH
    acc = jnp.dot(ysb_ref[...], w_ref[0:c1, :], preferred_element_type=F32)
    acc += jnp.dot(ycv_ref[...], w_ref[c1:c2, :], preferred_element_type=F32)
    acc += jnp.dot(ysg_ref[...], w_ref[c2:, :], preferred_element_type=F32)
    o_ref[...] = x_ref[...] + acc


def _out_proj(x, y_sb, y_cv, y_sg, w, *, tm, tn):
    t, d = x.shape
    k = w.shape[0]
    return pl.pallas_call(
        _outproj_kernel,
        grid=(t // tm, d // tn),
        in_specs=[
            pl.BlockSpec((tm, tn), lambda i, j: (i, j)),
            pl.BlockSpec((tm, SB_WIDTH), lambda i, j: (i, 0)),
            pl.BlockSpec((tm, CONV_WIDTH), lambda i, j: (i, 0)),
            pl.BlockSpec((tm, SG_WIDTH), lambda i, j: (i, 0)),
            pl.BlockSpec((k, tn), lambda i, j: (0, j)),
        ],
        out_specs=pl.BlockSpec((tm, tn), lambda i, j: (i, j)),
        out_shape=jax.ShapeDtypeStruct((t, d), F32),
        compiler_params=_params(2, 56),
        name="out_proj_residual",
    )(x, y_sb, y_cv, y_sg, w)


def _ffn_kernel(x_ref, halo_ref, g_ref, wg_ref, wu_ref, cw_ref, cb_ref, wd_ref, fin_ref,
                o_ref, h_ref, *, tm, tiles_per_seq, final):
    i = pl.program_id(0)
    j = pl.program_id(1)
    hr = BF16_ROWS

    @pl.when(j == 0)
    def _():
        x = x_ref[...]
        g = g_ref[...]
        h_ref[hr:, :] = _rms(x, g).astype(BF16)
        first = (i % tiles_per_seq) == 0
        hh = jnp.where(first, 0.0, _rms(halo_ref[...], g))
        h_ref[0:hr, :] = jnp.concatenate([jnp.zeros_like(hh), hh], axis=0).astype(BF16)
        o_ref[...] = x

    gate = jnp.dot(h_ref[...], wg_ref[...], preferred_element_type=F32)
    up = jnp.dot(h_ref[hr:, :], wu_ref[...], preferred_element_type=F32)
    gc = cb_ref[...] + gate[hr:, :] * cw_ref[2:3, :]
    gc = gc + gate[hr - 1:hr - 1 + tm, :] * cw_ref[1:2, :]
    gc = gc + gate[hr - 2:hr - 2 + tm, :] * cw_ref[0:1, :]
    act = (gc * jax.nn.sigmoid(gc) * up).astype(BF16)
    o_ref[...] += jnp.dot(act, wd_ref[...], preferred_element_type=F32)

    if final:
        @pl.when(j == pl.num_programs(1) - 1)
        def _():
            o_ref[...] = _rms(o_ref[...], fin_ref[...])


def _ffn(x, gain, wg, wu, cw, cb, wd, fin, *, seq, tm, fc, final):
    t, d = x.shape
    f = wg.shape[1]
    halo_blocks = tm // SUBLANES
    return pl.pallas_call(
        functools.partial(_ffn_kernel, tm=tm, tiles_per_seq=seq // tm, final=final),
        grid=(t // tm, f // fc),
        in_specs=[
            pl.BlockSpec((tm, d), lambda i, j: (i, 0)),
            pl.BlockSpec((SUBLANES, d), lambda i, j: (jnp.maximum(i * halo_blocks - 1, 0), 0)),
            pl.BlockSpec((1, d), lambda i, j: (0, 0)),
            pl.BlockSpec((d, fc), lambda i, j: (0, j)),
            pl.BlockSpec((d, fc), lambda i, j: (0, j)),
            pl.BlockSpec((FFN_CONV_KERNEL, fc), lambda i, j: (0, j)),
            pl.BlockSpec((1, fc), lambda i, j: (0, j)),
            pl.BlockSpec((fc, d), lambda i, j: (j, 0)),
            pl.BlockSpec((1, d), lambda i, j: (0, 0)),
        ],
        out_specs=pl.BlockSpec((tm, d), lambda i, j: (i, 0)),
        out_shape=jax.ShapeDtypeStruct((t, d), F32),
        scratch_shapes=[pltpu.VMEM((BF16_ROWS + tm, d), BF16)],
        compiler_params=_params(2, 56),
        name="gated_conv_ffn",
    )(x, x, gain.reshape(1, d), wg, wu, cw, cb.reshape(1, f), wd, fin.reshape(1, d))


def _pick(n, pref):
    t = min(pref, n)
    while n % t:
        t //= 2
    return t


def _cast_kernel(w_ref, o_ref):
    o_ref[...] = w_ref[...].astype(o_ref.dtype)


def _first_layer_to_bf16(w, *, tr=512):
    _, k, n = w.shape
    tr = _pick(k, tr)
    return pl.pallas_call(
        _cast_kernel,
        grid=(k // tr,),
        in_specs=[pl.BlockSpec((None, tr, n), lambda i: (0, i, 0))],
        out_specs=pl.BlockSpec((tr, n), lambda i: (i, 0)),
        out_shape=jax.ShapeDtypeStruct((k, n), BF16),
        compiler_params=_params(1, 48),
        name="weights_to_bf16",
    )(w)


def kernel(x, mix_norm, w_in, conv_w, conv_b, conv_ln_g, conv_ln_b, sg_v_norm, sg_w, sg_b,
           merge_norm, w_out, ffn_norm, w_gate, w_up, ffn_conv_w, ffn_conv_b, w_down, final_norm):
    b, s, d = x.shape
    t = b * s
    depth = w_in.shape[0]
    xf = x.reshape(t, d)
    c1 = SB_WIDTH
    c2 = SB_WIDTH + CONV_WIDTH
    w_in_b = _first_layer_to_bf16(w_in)
    for l in range(depth):
        qkv, rest = _proj(xf, mix_norm[l], w_in_b, tm=_pick(t, 1024), tn=1024)
        mg = merge_norm[l]
        cast_jobs = [(w_out, l), (w_gate, l), (w_up, l), (w_down, l)]
        if l + 1 < depth:
            cast_jobs.append((w_in, l + 1))
        y_sb, cast = _attention(qkv.reshape(b, s, -1), mg[:c1], cast_jobs, tq=_pick(s, 256),
                                tk=_pick(s, 256), npairs=4)
        w_out_b, w_gate_b, w_up_b, w_down_b = cast[:4]
        if l + 1 < depth:
            w_in_b = cast[4]
        y_cv = _conformer_conv(rest.reshape(b, s, -1), conv_w[l], conv_b[l], conv_ln_g[l],
                               conv_ln_b[l], mg[c1:c2])
        b_full = jnp.broadcast_to(sg_b[l][:, :, None], (SG_HEADS, SG_CHUNK, SG_HEAD_DIM))
        y_sg = _spatial_gating(rest, sg_v_norm[l], sg_w[l], b_full, mg[c2:], tm=_pick(s, 512))
        x1 = _out_proj(xf, y_sb.reshape(t, c1), y_cv.reshape(t, CONV_WIDTH), y_sg,
                       w_out_b, tm=_pick(t, 512), tn=d)
        xf = _ffn(x1, ffn_norm[l], w_gate_b, w_up_b, ffn_conv_w[l], ffn_conv_b[l], w_down_b,
                  final_norm, seq=s, tm=_pick(s, 1024), fc=512, final=(l == depth - 1))
    return xf.reshape(b, s, d)
```

```python
import functools

import jax
import jax.numpy as jnp
from jax import lax
from jax.experimental import pallas as pl
from jax.experimental.pallas import tpu as pltpu

F32 = jnp.float32
BF16 = jnp.bfloat16

EPS = 1e-6
SB_HEAD_DIM = 64
SB_HEADS = 16
SB_WIDTH = SB_HEAD_DIM * SB_HEADS
CONV_GROUP_DIM = 64
CONV_WIDTH = 512
CONV_KERNEL = 31
SG_HEAD_DIM = 128
SG_HEADS = 4
SG_WIDTH = SG_HEAD_DIM * SG_HEADS
SG_CHUNK = 128
STREAM_CHUNK = 64
FFN_CONV_KERNEL = 3

CARRY_CUTOFF = 128.0

LANES = 128
SUBLANES = 8
BF16_ROWS = 16
V7X_VMEM_BYTES = 64 * 1024 * 1024


def _params(n_grid, vmem_mb):
    return pltpu.CompilerParams(
        dimension_semantics=("arbitrary",) * n_grid,
        vmem_limit_bytes=min(vmem_mb * 1024 * 1024, V7X_VMEM_BYTES - 4 * 1024 * 1024),
    )


def _rms(x, g):
    ms = jnp.mean(x * x, axis=-1, keepdims=True)
    return x * lax.rsqrt(ms + EPS) * g


def _half_lane_mean(x, low):
    s_low = jnp.sum(jnp.where(low, x, 0.0), axis=-1, keepdims=True)
    s_all = jnp.sum(x, axis=-1, keepdims=True)
    return jnp.where(low, s_low, s_all - s_low) * (1.0 / 64.0)


def _proj_kernel(x_ref, g_ref, w_ref, qkv_ref, rest_ref, h_ref, *, n_qkv_tiles):
    j = pl.program_id(1)

    @pl.when(j == 0)
    def _():
        h_ref[...] = _rms(x_ref[...], g_ref[...]).astype(BF16)

    def mm():
        return jnp.dot(h_ref[...], w_ref[...], preferred_element_type=F32)

    @pl.when(j < n_qkv_tiles)
    def _():
        qkv_ref[...] = mm().astype(BF16)

    @pl.when(j >= n_qkv_tiles)
    def _():
        rest_ref[...] = mm()


def _proj(x, gain, w, *, tm, tn):
    t, d = x.shape
    n_out = w.shape[1]
    n_qkv = 3 * SB_WIDTH
    n_rest = n_out - n_qkv
    nq = n_qkv // tn
    return pl.pallas_call(
        functools.partial(_proj_kernel, n_qkv_tiles=nq),
        grid=(t // tm, n_out // tn),
        in_specs=[
            pl.BlockSpec((tm, d), lambda i, j: (i, 0)),
            pl.BlockSpec((1, d), lambda i, j: (0, 0)),
            pl.BlockSpec((d, tn), lambda i, j: (0, j)),
        ],
        out_specs=[
            pl.BlockSpec((tm, tn), lambda i, j: (i, jnp.minimum(j, nq - 1))),
            pl.BlockSpec((tm, tn), lambda i, j: (i, jnp.maximum(j - nq, 0))),
        ],
        out_shape=[
            jax.ShapeDtypeStruct((t, n_qkv), BF16),
            jax.ShapeDtypeStruct((t, n_rest), F32),
        ],
        scratch_shapes=[pltpu.VMEM((tm, d), BF16)],
        compiler_params=_params(2, 56),
        name="rms_in_proj",
    )(x, gain.reshape(1, d), w)


def _attn_kernel(q_ref, k_ref, v_ref, g_ref, *refs, tq, tk, npairs, n_cast):
    cast_in = refs[:n_cast]
    o_ref = refs[n_cast]
    cast_out = refs[n_cast + 1:2 * n_cast + 1]
    acc_ref, car_ref = refs[2 * n_cast + 1:]
    for src, dst in zip(cast_in, cast_out):
        dst[...] = src[...].astype(dst.dtype)

    qi = pl.program_id(2)
    n_diag = tq // tk
    nh = 2 * npairs
    low = lax.broadcasted_iota(jnp.int32, (tq, LANES), 1) < SB_HEAD_DIM
    scale = SB_HEAD_DIM ** -0.5
    lanes = [slice(p * LANES, (p + 1) * LANES) for p in range(npairs)]
    q_heads = []
    for p in range(npairs):
        q = q_ref[:, lanes[p]]
        zero = jnp.zeros_like(q)
        q_heads += [jnp.where(low, q, zero) * scale, jnp.where(low, zero, q) * scale]
    tri = (lax.broadcasted_iota(jnp.int32, (tk, tk), 0)
           >= lax.broadcasted_iota(jnp.int32, (tk, tk), 1)).astype(BF16)
    row = lax.broadcasted_iota(jnp.int32, (tq, tk), 0)
    col = lax.broadcasted_iota(jnp.int32, (tq, tk), 1)
    sign = jnp.uint32(0x80000000)

    acc_ref[...] = jnp.zeros_like(acc_ref)
    car_ref[...] = jnp.zeros_like(car_ref)

    def chunk(start, causal):
        kcs = [k_ref[pl.ds(start, tk), lanes[p]] for p in range(npairs)]
        vcs = [v_ref[pl.ds(start, tk), lanes[p]] for p in range(npairs)]
        zs = [lax.dot_general(q_heads[h], kcs[h // 2], (((1,), (1,)), ((), ())),
                              preferred_element_type=F32) for h in range(nh)]
        sps = []
        for z in zs:
            neg_abs = pltpu.bitcast(pltpu.bitcast(z, jnp.uint32) | sign, F32)
            sp = jnp.maximum(z, 0.0) + jnp.log(1.0 + jnp.exp(neg_abs))
            if causal is not None:
                sp = jnp.where(causal, sp, 0.0)
            sps.append(sp)
        withins = [jnp.dot(sp.astype(BF16), tri, preferred_element_type=F32) for sp in sps]
        ws = []
        for h in range(nh):
            car = car_ref[h]
            total = jnp.concatenate([car] * (tk // LANES), axis=-1) + withins[h]
            w = jnp.exp(zs[h] - total)
            if causal is not None:
                w = jnp.where(causal, w, 0.0)
            ws.append(w.astype(BF16))
            car_ref[h] = car + jnp.sum(sps[h], axis=-1, keepdims=True)
        for h in range(nh):
            acc_ref[h] += jnp.dot(ws[h], vcs[h // 2], preferred_element_type=F32)

    for d in range(n_diag - 1, -1, -1):
        chunk(pl.multiple_of(qi * tq + d * tk, tk), (d * tk + col) < row)

    n_off = qi * n_diag

    def more(state):
        it, smallest = state
        return jnp.logical_and(it < n_off, smallest <= CARRY_CUTOFF)

    def body(state):
        it, _ = state
        chunk(pl.multiple_of((n_off - 1 - it) * tk, tk), None)
        car = car_ref[0]
        for h in range(1, nh):
            car = jnp.minimum(car, car_ref[h])
        smallest = jnp.min(jnp.min(car, axis=0, keepdims=True), axis=1, keepdims=True)
        return it + 1, smallest[0, 0]

    lax.while_loop(more, body, (jnp.int32(0), jnp.float32(0.0)))

    for p in range(npairs):
        y = jnp.where(low, acc_ref[2 * p], acc_ref[2 * p + 1])
        ms = _half_lane_mean(y * y, low)
        o_ref[:, lanes[p]] = (y * lax.rsqrt(ms + EPS) * g_ref[:, lanes[p]]).astype(o_ref.dtype)


def _attention(qkv, gain, cast_jobs, *, tq, tk, npairs):
    b, s, _ = qkv.shape
    width = LANES * npairs
    n_groups = SB_WIDTH // width
    n_q = s // tq
    n_steps = b * n_groups * n_q

    def step(bi, hg, qi):
        return (bi * n_groups + hg) * n_q + qi

    cast_specs_in, cast_specs_out, cast_shapes = [], [], []
    for w, layer in cast_jobs:
        _, k, n = w.shape
        n_row = next(a for a in range(n_steps, 0, -1)
                     if n_steps % a == 0 and k % (a * BF16_ROWS) == 0
                     and n % ((n_steps // a) * LANES) == 0)
        n_col = n_steps // n_row
        blk = (k // n_row, n // n_col)
        cast_specs_in.append(pl.BlockSpec(
            (None,) + blk,
            lambda bi, hg, qi, layer=layer, n_col=n_col:
            (layer, step(bi, hg, qi) // n_col, step(bi, hg, qi) % n_col)))
        cast_specs_out.append(pl.BlockSpec(
            blk, lambda bi, hg, qi, n_col=n_col:
            (step(bi, hg, qi) // n_col, step(bi, hg, qi) % n_col)))
        cast_shapes.append(jax.ShapeDtypeStruct((k, n), BF16))

    outs = pl.pallas_call(
        functools.partial(_attn_kernel, tq=tq, tk=tk, npairs=npairs, n_cast=len(cast_jobs)),
        grid=(b, n_groups, n_q),
        in_specs=[
            pl.BlockSpec((None, tq, width), lambda bi, hg, qi: (bi, qi, hg)),
            pl.BlockSpec((None, s, width), lambda bi, hg, qi: (bi, 0, n_groups + hg)),
            pl.BlockSpec((None, s, width), lambda bi, hg, qi: (bi, 0, 2 * n_groups + hg)),
            pl.BlockSpec((1, width), lambda bi, hg, qi: (0, hg)),
        ] + cast_specs_in,
        out_specs=[pl.BlockSpec((None, tq, width), lambda bi, hg, qi: (bi, qi, hg))]
        + cast_specs_out,
        out_shape=[jax.ShapeDtypeStruct((b, s, SB_WIDTH), BF16)] + cast_shapes,
        scratch_shapes=[
            pltpu.VMEM((2 * npairs, tq, LANES), F32),
            pltpu.VMEM((2 * npairs, tq, LANES), F32),
        ],
        compiler_params=_params(3, 48),
        name="stick_breaking_attn",
    )(qkv, qkv, qkv, gain.reshape(1, -1), *[w for w, _ in cast_jobs])
    return outs[0], list(outs[1:])


def _conv_kernel(a_ref, gate_ref, cw_ref, cb_ref, lg_ref, lb_ref, mg_ref, o_ref, hpad_ref,
                 *, s, tr, pad):
    low = lax.broadcasted_iota(jnp.int32, (tr, LANES), 1) < CONV_GROUP_DIM
    hpad_ref[0:pad, :] = jnp.zeros((pad, LANES), F32)

    def glu(it, carry):
        r0 = pl.multiple_of(it * tr, tr)
        a = a_ref[pl.ds(r0, tr), :]
        gt = gate_ref[pl.ds(r0, tr), :]
        hpad_ref[pl.ds(pad + r0, tr), :] = a * jax.nn.sigmoid(gt)
        return carry

    lax.fori_loop(0, s // tr, glu, 0)

    cb = cb_ref[...]
    lg = lg_ref[...]
    lb = lb_ref[...]
    mg = mg_ref[...]

    def tile(it, carry):
        r0 = pl.multiple_of(it * tr, tr)
        acc = jnp.broadcast_to(cb, (tr, LANES))
        for k in range(CONV_KERNEL):
            off = pad - (CONV_KERNEL - 1) + k
            acc = acc + hpad_ref[pl.ds(r0 + off, tr), :] * cw_ref[pl.ds(k, 1), :]
        mu = _half_lane_mean(acc, low)
        dlt = acc - mu
        var = _half_lane_mean(dlt * dlt, low)
        y = dlt * lax.rsqrt(var + EPS) * lg + lb
        y = y * jax.nn.sigmoid(y)
        ms = _half_lane_mean(y * y, low)
        o_ref[pl.ds(r0, tr), :] = (y * lax.rsqrt(ms + EPS) * mg).astype(o_ref.dtype)
        return carry

    lax.fori_loop(0, s // tr, tile, 0, unroll=2)


def _conformer_conv(rest, cw, cb, lg, lb, mg, *, tr=128, pad=32):
    b, s, _ = rest.shape
    nblk = CONV_WIDTH // LANES
    vec = lambda: pl.BlockSpec((1, LANES), lambda bi, c: (0, c))
    return pl.pallas_call(
        functools.partial(_conv_kernel, s=s, tr=tr, pad=pad),
        grid=(b, nblk),
        in_specs=[
            pl.BlockSpec((None, s, LANES), lambda bi, c: (bi, 0, c)),
            pl.BlockSpec((None, s, LANES), lambda bi, c: (bi, 0, nblk + c)),
            pl.BlockSpec((CONV_KERNEL, LANES), lambda bi, c: (0, c)),
            vec(), vec(), vec(), vec(),
        ],
        out_specs=pl.BlockSpec((None, s, LANES), lambda bi, c: (bi, 0, c)),
        out_shape=jax.ShapeDtypeStruct((b, s, CONV_WIDTH), BF16),
        scratch_shapes=[pltpu.VMEM((s + pad, LANES), F32)],
        compiler_params=_params(2, 32),
        name="conformer_conv",
    )(rest, rest, cw, cb.reshape(1, -1), lg.reshape(1, -1), lb.reshape(1, -1), mg.reshape(1, -1))


def _sg_kernel(u_ref, v_ref, vg_ref, w_ref, b_ref, mg_ref, o_ref, *, tm):
    u = jax.nn.gelu(u_ref[...])
    v = jax.nn.gelu(v_ref[...])
    vg = vg_ref[...]
    mg = mg_ref[...]
    ri = lax.broadcasted_iota(jnp.int32, (SG_CHUNK, SG_CHUNK), 0) // STREAM_CHUNK
    ci = lax.broadcasted_iota(jnp.int32, (SG_CHUNK, SG_CHUNK), 1) // STREAM_CHUNK
    allowed = ci <= ri
    for h in range(SG_HEADS):
        sl = slice(h * SG_HEAD_DIM, (h + 1) * SG_HEAD_DIM)
        vn = _rms(v[:, sl], vg[:, sl]).astype(BF16)
        wh = jnp.where(allowed, w_ref[h], 0.0).astype(BF16)
        for n in range(tm // SG_CHUNK):
            rs = slice(n * SG_CHUNK, (n + 1) * SG_CHUNK)
            mixed = jnp.dot(wh, vn[rs], preferred_element_type=F32) + b_ref[h]
            o_ref[rs, sl] = _rms(u[rs, sl] * mixed, mg[:, sl]).astype(o_ref.dtype)


def _spatial_gating(rest, vg, w, b_full, mg, *, tm):
    t, n_rest = rest.shape
    u_blk = (n_rest - 2 * SG_WIDTH) // SG_WIDTH
    return pl.pallas_call(
        functools.partial(_sg_kernel, tm=tm),
        grid=(t // tm,),
        in_specs=[
            pl.BlockSpec((tm, SG_WIDTH), lambda i: (i, u_blk)),
            pl.BlockSpec((tm, SG_WIDTH), lambda i: (i, u_blk + 1)),
            pl.BlockSpec((1, SG_WIDTH), lambda i: (0, 0)),
            pl.BlockSpec((SG_HEADS, SG_CHUNK, SG_CHUNK), lambda i: (0, 0, 0)),
            pl.BlockSpec((SG_HEADS, SG_CHUNK, SG_HEAD_DIM), lambda i: (0, 0, 0)),
            pl.BlockSpec((1, SG_WIDTH), lambda i: (0, 0)),
        ],
        out_specs=pl.BlockSpec((tm, SG_WIDTH), lambda i: (i, 0)),
        out_shape=jax.ShapeDtypeStruct((t, SG_WIDTH), BF16),
        compiler_params=_params(1, 32),
        name="spatial_gating",
    )(rest, rest, vg.reshape(1, -1), w, b_full, mg.reshape(1, -1))


def _outproj_kernel(x_ref, ysb_ref, ycv_ref, ysg_ref, w_ref, o_ref):
    c1 = SB_WIDTH
    c2 = SB_WIDTH + CONV_WIDTH
    acc = jnp.dot(ysb_ref[...], w_ref[0:c1, :], preferred_element_type=F32)
    acc += jnp.dot(ycv_ref[...], w_ref[c1:c2, :], preferred_element_type=F32)
    acc += jnp.dot(ysg_ref[...], w_ref[c2:, :], preferred_element_type=F32)
    o_ref[...] = x_ref[...] + acc


def _out_proj(x, y_sb, y_cv, y_sg, w, *, tm, tn):
    t, d = x.shape
    k = w.shape[0]
    return pl.pallas_call(
        _outproj_kernel,
        grid=(t // tm, d // tn),
        in_specs=[
            pl.BlockSpec((tm, tn), lambda i, j: (i, j)),
            pl.BlockSpec((tm, SB_WIDTH), lambda i, j: (i, 0)),
            pl.BlockSpec((tm, CONV_WIDTH), lambda i, j: (i, 0)),
            pl.BlockSpec((tm, SG_WIDTH), lambda i, j: (i, 0)),
            pl.BlockSpec((k, tn), lambda i, j: (0, j)),
        ],
        out_specs=pl.BlockSpec((tm, tn), lambda i, j: (i, j)),
        out_shape=jax.ShapeDtypeStruct((t, d), F32),
        compiler_params=_params(2, 56),
        name="out_proj_residual",
    )(x, y_sb, y_cv, y_sg, w)


def _ffn_kernel(x_ref, halo_ref, g_ref, wg_ref, wu_ref, cw_ref, cb_ref, wd_ref, fin_ref,
                o_ref, h_ref, *, tm, tiles_per_seq, final):
    i = pl.program_id(0)
    j = pl.program_id(1)
    hr = BF16_ROWS

    @pl.when(j == 0)
    def _():
        x = x_ref[...]
        g = g_ref[...]
        h_ref[hr:, :] = _rms(x, g).astype(BF16)
        first = (i % tiles_per_seq) == 0
        hh = jnp.where(first, 0.0, _rms(halo_ref[...], g))
        h_ref[0:hr, :] = jnp.concatenate([jnp.zeros_like(hh), hh], axis=0).astype(BF16)
        o_ref[...] = x

    gate = jnp.dot(h_ref[...], wg_ref[...], preferred_element_type=F32)
    up = jnp.dot(h_ref[hr:, :], wu_ref[...], preferred_element_type=F32)
    gc = cb_ref[...] + gate[hr:, :] * cw_ref[2:3, :]
    gc = gc + gate[hr - 1:hr - 1 + tm, :] * cw_ref[1:2, :]
    gc = gc + gate[hr - 2:hr - 2 + tm, :] * cw_ref[0:1, :]
    act = (gc * jax.nn.sigmoid(gc) * up).astype(BF16)
    o_ref[...] += jnp.dot(act, wd_ref[...], preferred_element_type=F32)

    if final:
        @pl.when(j == pl.num_programs(1) - 1)
        def _():
            o_ref[...] = _rms(o_ref[...], fin_ref[...])


def _ffn(x, gain, wg, wu, cw, cb, wd, fin, *, seq, tm, fc, final):
    t, d = x.shape
    f = wg.shape[1]
    halo_blocks = tm // SUBLANES
    return pl.pallas_call(
        functools.partial(_ffn_kernel, tm=tm, tiles_per_seq=seq // tm, final=final),
        grid=(t // tm, f // fc),
        in_specs=[
            pl.BlockSpec((tm, d), lambda i, j: (i, 0)),
            pl.BlockSpec((SUBLANES, d), lambda i, j: (jnp.maximum(i * halo_blocks - 1, 0), 0)),
            pl.BlockSpec((1, d), lambda i, j: (0, 0)),
            pl.BlockSpec((d, fc), lambda i, j: (0, j)),
            pl.BlockSpec((d, fc), lambda i, j: (0, j)),
            pl.BlockSpec((FFN_CONV_KERNEL, fc), lambda i, j: (0, j)),
            pl.BlockSpec((1, fc), lambda i, j: (0, j)),
            pl.BlockSpec((fc, d), lambda i, j: (j, 0)),
            pl.BlockSpec((1, d), lambda i, j: (0, 0)),
        ],
        out_specs=pl.BlockSpec((tm, d), lambda i, j: (i, 0)),
        out_shape=jax.ShapeDtypeStruct((t, d), F32),
        scratch_shapes=[pltpu.VMEM((BF16_ROWS + tm, d), BF16)],
        compiler_params=_params(2, 56),
        name="gated_conv_ffn",
    )(x, x, gain.reshape(1, d), wg, wu, cw, cb.reshape(1, f), wd, fin.reshape(1, d))


def _pick(n, pref):
    t = min(pref, n)
    while n % t:
        t //= 2
    return t


def _cast_kernel(w_ref, o_ref):
    o_ref[...] = w_ref[...].astype(o_ref.dtype)


def _first_layer_to_bf16(w, *, tr=512):
    _, k, n = w.shape
    tr = _pick(k, tr)
    return pl.pallas_call(
        _cast_kernel,
        grid=(k // tr,),
        in_specs=[pl.BlockSpec((None, tr, n), lambda i: (0, i, 0))],
        out_specs=pl.BlockSpec((tr, n), lambda i: (i, 0)),
        out_shape=jax.ShapeDtypeStruct((k, n), BF16),
        compiler_params=_params(1, 48),
        name="weights_to_bf16",
    )(w)


def kernel(x, mix_norm, w_in, conv_w, conv_b, conv_ln_g, conv_ln_b, sg_v_norm, sg_w, sg_b,
           merge_norm, w_out, ffn_norm, w_gate, w_up, ffn_conv_w, ffn_conv_b, w_down, final_norm):
    b, s, d = x.shape
    t = b * s
    depth = w_in.shape[0]
    xf = x.reshape(t, d)
    c1 = SB_WIDTH
    c2 = SB_WIDTH + CONV_WIDTH
    w_in_b = _first_layer_to_bf16(w_in)
    for l in range(depth):
        qkv, rest = _proj(xf, mix_norm[l], w_in_b, tm=_pick(t, 1024), tn=1024)
        mg = merge_norm[l]
        cast_jobs = [(w_out, l), (w_gate, l), (w_up, l), (w_down, l)]
        if l + 1 < depth:
            cast_jobs.append((w_in, l + 1))
        y_sb, cast = _attention(qkv.reshape(b, s, -1), mg[:c1], cast_jobs, tq=_pick(s, 256),
                                tk=_pick(s, 256), npairs=4)
        w_out_b, w_gate_b, w_up_b, w_down_b = cast[:4]
        if l + 1 < depth:
            w_in_b = cast[4]
        y_cv = _conformer_conv(rest.reshape(b, s, -1), conv_w[l], conv_b[l], conv_ln_g[l],
                               conv_ln_b[l], mg[c1:c2])
        b_full = jnp.broadcast_to(sg_b[l][:, :, None], (SG_HEADS, SG_CHUNK, SG_HEAD_DIM))
        y_sg = _spatial_gating(rest, sg_v_norm[l], sg_w[l], b_full, mg[c2:], tm=_pick(s, 512))
        x1 = _out_proj(xf, y_sb.reshape(t, c1), y_cv.reshape(t, CONV_WIDTH), y_sg,
                       w_out_b, tm=_pick(t, 512), tn=d)
        xf = _ffn(x1, ffn_norm[l], w_gate_b, w_up_b, ffn_conv_w[l], ffn_conv_b[l], w_down_b,
                  final_norm, seq=s, tm=_pick(s, 1024), fc=512, final=(l == depth - 1))
    return xf.reshape(b, s, d)
```

```python
import functools

import jax
import jax.numpy as jnp
from jax import lax
from jax.experimental import pallas as pl
from jax.experimental.pallas import tpu as pltpu

F32 = jnp.float32
BF16 = jnp.bfloat16

EPS = 1e-6
SB_HEAD_DIM = 64
SB_HEADS = 16
SB_WIDTH = SB_HEAD_DIM * SB_HEADS
CONV_GROUP_DIM = 64
CONV_WIDTH = 512
CONV_KERNEL = 31
SG_HEAD_DIM = 128
SG_HEADS = 4
SG_WIDTH = SG_HEAD_DIM * SG_HEADS
SG_CHUNK = 128
STREAM_CHUNK = 64
FFN_CONV_KERNEL = 3

CARRY_CUTOFF = 128.0

LANES = 128
SUBLANES = 8
BF16_ROWS = 16
V7X_VMEM_BYTES = 64 * 1024 * 1024


def _params(n_grid, vmem_mb):
    return pltpu.CompilerParams(
        dimension_semantics=("arbitrary",) * n_grid,
        vmem_limit_bytes=min(vmem_mb * 1024 * 1024, V7X_VMEM_BYTES - 4 * 1024 * 1024),
    )


def _rms(x, g):
    ms = jnp.mean(x * x, axis=-1, keepdims=True)
    return x * lax.rsqrt(ms + EPS) * g


def _half_lane_mean(x, low):
    s_low = jnp.sum(jnp.where(low, x, 0.0), axis=-1, keepdims=True)
    s_all = jnp.sum(x, axis=-1, keepdims=True)
    return jnp.where(low, s_low, s_all - s_low) * (1.0 / 64.0)


def _proj_kernel(x_ref, g_ref, w_ref, qkv_ref, rest_ref):
    n_qkv = qkv_ref.shape[1]
    h = _rms(x_ref[...], g_ref[...]).astype(BF16)
    qkv_ref[...] = jnp.dot(h, w_ref[:, :n_qkv], preferred_element_type=F32).astype(BF16)
    rest_ref[...] = jnp.dot(h, w_ref[:, n_qkv:], preferred_element_type=F32)


def _proj(x, gain, w, *, tm):
    t, d = x.shape
    n_out = w.shape[1]
    n_qkv = 3 * SB_WIDTH
    n_rest = n_out - n_qkv
    return pl.pallas_call(
        _proj_kernel,
        grid=(t // tm,),
        in_specs=[
            pl.BlockSpec((tm, d), lambda i: (i, 0)),
            pl.BlockSpec((1, d), lambda i: (0, 0)),
            pl.BlockSpec((d, n_out), lambda i: (0, 0), pipeline_mode=pl.Buffered(1)),
        ],
        out_specs=[
            pl.BlockSpec((tm, n_qkv), lambda i: (i, 0)),
            pl.BlockSpec((tm, n_rest), lambda i: (i, 0)),
        ],
        out_shape=[
            jax.ShapeDtypeStruct((t, n_qkv), BF16),
            jax.ShapeDtypeStruct((t, n_rest), F32),
        ],
        compiler_params=_params(1, 60),
        name="rms_in_proj",
    )(x, gain.reshape(1, d), w)


def _attn_kernel(q_ref, k_ref, v_ref, g_ref, *refs, tq, tk, npairs, n_cast):
    cast_in = refs[:n_cast]
    o_ref = refs[n_cast]
    cast_out = refs[n_cast + 1:2 * n_cast + 1]
    acc_ref, car_ref = refs[2 * n_cast + 1:]
    for src, dst in zip(cast_in, cast_out):
        dst[...] = src[...].astype(dst.dtype)

    qi = pl.program_id(2)
    n_diag = tq // tk
    nh = 2 * npairs
    low = lax.broadcasted_iota(jnp.int32, (tq, LANES), 1) < SB_HEAD_DIM
    scale = SB_HEAD_DIM ** -0.5
    lanes = [slice(p * LANES, (p + 1) * LANES) for p in range(npairs)]
    q_heads = []
    for p in range(npairs):
        q = q_ref[:, lanes[p]]
        zero = jnp.zeros_like(q)
        q_heads += [jnp.where(low, q, zero) * scale, jnp.where(low, zero, q) * scale]
    tri = (lax.broadcasted_iota(jnp.int32, (tk, tk), 0)
           >= lax.broadcasted_iota(jnp.int32, (tk, tk), 1)).astype(BF16)
    row = lax.broadcasted_iota(jnp.int32, (tq, tk), 0)
    col = lax.broadcasted_iota(jnp.int32, (tq, tk), 1)
    sign = jnp.uint32(0x80000000)

    acc_ref[...] = jnp.zeros_like(acc_ref)
    car_ref[...] = jnp.zeros_like(car_ref)

    def chunk(start, causal):
        kcs = [k_ref[pl.ds(start, tk), lanes[p]] for p in range(npairs)]
        vcs = [v_ref[pl.ds(start, tk), lanes[p]] for p in range(npairs)]
        zs = [lax.dot_general(q_heads[h], kcs[h // 2], (((1,), (1,)), ((), ())),
                              preferred_element_type=F32) for h in range(nh)]
        sps = []
        for z in zs:
            neg_abs = pltpu.bitcast(pltpu.bitcast(z, jnp.uint32) | sign, F32)
            sp = jnp.maximum(z, 0.0) + jnp.log(1.0 + jnp.exp(neg_abs))
            if causal is not None:
                sp = jnp.where(causal, sp, 0.0)
            sps.append(sp)
        withins = [jnp.dot(sp.astype(BF16), tri, preferred_element_type=F32) for sp in sps]
        ws = []
        for h in range(nh):
            car = car_ref[h]
            total = jnp.concatenate([car] * (tk // LANES), axis=-1) + withins[h]
            w = jnp.exp(zs[h] - total)
            if causal is not None:
                w = jnp.where(causal, w, 0.0)
            ws.append(w.astype(BF16))
            car_ref[h] = car + jnp.sum(sps[h], axis=-1, keepdims=True)
        for h in range(nh):
            acc_ref[h] += jnp.dot(ws[h], vcs[h // 2], preferred_element_type=F32)

    for d in range(n_diag - 1, -1, -1):
        chunk(pl.multiple_of(qi * tq + d * tk, tk), (d * tk + col) < row)

    n_off = qi * n_diag

    def more(state):
        it, smallest = state
        return jnp.logical_and(it < n_off, smallest <= CARRY_CUTOFF)

    def body(state):
        it, _ = state
        chunk(pl.multiple_of((n_off - 1 - it) * tk, tk), None)
        car = car_ref[0]
        for h in range(1, nh):
            car = jnp.minimum(car, car_ref[h])
        smallest = jnp.min(jnp.min(car, axis=0, keepdims=True), axis=1, keepdims=True)
        return it + 1, smallest[0, 0]

    lax.while_loop(more, body, (jnp.int32(0), jnp.float32(0.0)))

    for p in range(npairs):
        y = jnp.where(low, acc_ref[2 * p], acc_ref[2 * p + 1])
        ms = _half_lane_mean(y * y, low)
        o_ref[:, lanes[p]] = (y * lax.rsqrt(ms + EPS) * g_ref[:, lanes[p]]).astype(o_ref.dtype)


def _attention(qkv, gain, cast_jobs, *, tq, tk, npairs):
    b, s, _ = qkv.shape
    width = LANES * npairs
    n_groups = SB_WIDTH // width
    n_q = s // tq
    n_steps = b * n_groups * n_q

    def step(bi, hg, qi):
        return (bi * n_groups + hg) * n_q + qi

    cast_specs_in, cast_specs_out, cast_shapes = [], [], []
    for w, layer in cast_jobs:
        _, k, n = w.shape
        n_row = next(a for a in range(n_steps, 0, -1)
                     if n_steps % a == 0 and k % (a * BF16_ROWS) == 0
                     and n % ((n_steps // a) * LANES) == 0)
        n_col = n_steps // n_row
        blk = (k // n_row, n // n_col)
        cast_specs_in.append(pl.BlockSpec(
            (None,) + blk,
            lambda bi, hg, qi, layer=layer, n_col=n_col:
            (layer, step(bi, hg, qi) // n_col, step(bi, hg, qi) % n_col)))
        cast_specs_out.append(pl.BlockSpec(
            blk, lambda bi, hg, qi, n_col=n_col:
            (step(bi, hg, qi) // n_col, step(bi, hg, qi) % n_col)))
        cast_shapes.append(jax.ShapeDtypeStruct((k, n), BF16))

    outs = pl.pallas_call(
        functools.partial(_attn_kernel, tq=tq, tk=tk, npairs=npairs, n_cast=len(cast_jobs)),
        grid=(b, n_groups, n_q),
        in_specs=[
            pl.BlockSpec((None, tq, width), lambda bi, hg, qi: (bi, qi, hg)),
            pl.BlockSpec((None, s, width), lambda bi, hg, qi: (bi, 0, n_groups + hg)),
            pl.BlockSpec((None, s, width), lambda bi, hg, qi: (bi, 0, 2 * n_groups + hg)),
            pl.BlockSpec((1, width), lambda bi, hg, qi: (0, hg)),
        ] + cast_specs_in,
        out_specs=[pl.BlockSpec((None, tq, width), lambda bi, hg, qi: (bi, qi, hg))]
        + cast_specs_out,
        out_shape=[jax.ShapeDtypeStruct((b, s, SB_WIDTH), BF16)] + cast_shapes,
        scratch_shapes=[
            pltpu.VMEM((2 * npairs, tq, LANES), F32),
            pltpu.VMEM((2 * npairs, tq, LANES), F32),
        ],
        compiler_params=_params(3, 48),
        name="stick_breaking_attn",
    )(qkv, qkv, qkv, gain.reshape(1, -1), *[w for w, _ in cast_jobs])
    return outs[0], list(outs[1:])


def _conv_kernel(a_ref, gate_ref, cw_ref, cb_ref, lg_ref, lb_ref, mg_ref, o_ref, hpad_ref,
                 *, s, tr, pad):
    low = lax.broadcasted_iota(jnp.int32, (tr, LANES), 1) < CONV_GROUP_DIM
    hpad_ref[0:pad, :] = jnp.zeros((pad, LANES), F32)

    def glu(it, carry):
        r0 = pl.multiple_of(it * tr, tr)
        a = a_ref[pl.ds(r0, tr), :]
        gt = gate_ref[pl.ds(r0, tr), :]
        hpad_ref[pl.ds(pad + r0, tr), :] = a * jax.nn.sigmoid(gt)
        return carry

    lax.fori_loop(0, s // tr, glu, 0)

    cb = cb_ref[...]
    lg = lg_ref[...]
    lb = lb_ref[...]
    mg = mg_ref[...]

    def tile(it, carry):
        r0 = pl.multiple_of(it * tr, tr)
        acc = jnp.broadcast_to(cb, (tr, LANES))
        for k in range(CONV_KERNEL):
            off = pad - (CONV_KERNEL - 1) + k
            acc = acc + hpad_ref[pl.ds(r0 + off, tr), :] * cw_ref[pl.ds(k, 1), :]
        mu = _half_lane_mean(acc, low)
        dlt = acc - mu
        var = _half_lane_mean(dlt * dlt, low)
        y = dlt * lax.rsqrt(var + EPS) * lg + lb
        y = y * jax.nn.sigmoid(y)
        ms = _half_lane_mean(y * y, low)
        o_ref[pl.ds(r0, tr), :] = (y * lax.rsqrt(ms + EPS) * mg).astype(o_ref.dtype)
        return carry

    lax.fori_loop(0, s // tr, tile, 0, unroll=4)


def _conformer_conv(rest, cw, cb, lg, lb, mg, *, tr=128, pad=32):
    b, s, _ = rest.shape
    nblk = CONV_WIDTH // LANES
    vec = lambda: pl.BlockSpec((1, LANES), lambda bi, c: (0, c))
    return pl.pallas_call(
        functools.partial(_conv_kernel, s=s, tr=tr, pad=pad),
        grid=(b, nblk),
        in_specs=[
            pl.BlockSpec((None, s, LANES), lambda bi, c: (bi, 0, c)),
            pl.BlockSpec((None, s, LANES), lambda bi, c: (bi, 0, nblk + c)),
            pl.BlockSpec((CONV_KERNEL, LANES), lambda bi, c: (0, c)),
            vec(), vec(), vec(), vec(),
        ],
        out_specs=pl.BlockSpec((None, s, LANES), lambda bi, c: (bi, 0, c)),
        out_shape=jax.ShapeDtypeStruct((b, s, CONV_WIDTH), BF16),
        scratch_shapes=[pltpu.VMEM((s + pad, LANES), F32)],
        compiler_params=_params(2, 32),
        name="conformer_conv",
    )(rest, rest, cw, cb.reshape(1, -1), lg.reshape(1, -1), lb.reshape(1, -1), mg.reshape(1, -1))


def _sg_kernel(u_ref, v_ref, vg_ref, w_ref, b_ref, mg_ref, o_ref, *, tm):
    u = jax.nn.gelu(u_ref[...])
    v = jax.nn.gelu(v_ref[...])
    vg = vg_ref[...]
    mg = mg_ref[...]
    ri = lax.broadcasted_iota(jnp.int32, (SG_CHUNK, SG_CHUNK), 0) // STREAM_CHUNK
    ci = lax.broadcasted_iota(jnp.int32, (SG_CHUNK, SG_CHUNK), 1) // STREAM_CHUNK
    allowed = ci <= ri
    for h in range(SG_HEADS):
        sl = slice(h * SG_HEAD_DIM, (h + 1) * SG_HEAD_DIM)
        vn = _rms(v[:, sl], vg[:, sl]).astype(BF16)
        wh = jnp.where(allowed, w_ref[h], 0.0).astype(BF16)
        for n in range(tm // SG_CHUNK):
            rs = slice(n * SG_CHUNK, (n + 1) * SG_CHUNK)
            mixed = jnp.dot(wh, vn[rs], preferred_element_type=F32) + b_ref[h]
            o_ref[rs, sl] = _rms(u[rs, sl] * mixed, mg[:, sl]).astype(o_ref.dtype)


def _spatial_gating(rest, vg, w, b_full, mg, *, tm):
    t, n_rest = rest.shape
    u_blk = (n_rest - 2 * SG_WIDTH) // SG_WIDTH
    return pl.pallas_call(
        functools.partial(_sg_kernel, tm=tm),
        grid=(t // tm,),
        in_specs=[
            pl.BlockSpec((tm, SG_WIDTH), lambda i: (i, u_blk)),
            pl.BlockSpec((tm, SG_WIDTH), lambda i: (i, u_blk + 1)),
            pl.BlockSpec((1, SG_WIDTH), lambda i: (0, 0)),
            pl.BlockSpec((SG_HEADS, SG_CHUNK, SG_CHUNK), lambda i: (0, 0, 0)),
            pl.BlockSpec((SG_HEADS, SG_CHUNK, SG_HEAD_DIM), lambda i: (0, 0, 0)),
            pl.BlockSpec((1, SG_WIDTH), lambda i: (0, 0)),
        ],
        out_specs=pl.BlockSpec((tm, SG_WIDTH), lambda i: (i, 0)),
        out_shape=jax.ShapeDtypeStruct((t, SG_WIDTH), BF16),
        compiler_params=_params(1, 32),
        name="spatial_gating",
    )(rest, rest, vg.reshape(1, -1), w, b_full, mg.reshape(1, -1))


def _outproj_kernel(x_ref, ysb_ref, ycv_ref, ysg_ref, w_ref, o_ref):
    c1 = SB_WIDTH
    c2 = SB_WIDTH + CONV_WIDTH
    acc = jnp.dot(ysb_ref[...], w_ref[0:c1, :], preferred_element_type=F32)
    acc += jnp.dot(ycv_ref[...], w_ref[c1:c2, :], preferred_element_type=F32)
    acc += jnp.dot(ysg_ref[...], w_ref[c2:, :], preferred_element_type=F32)
    o_ref[...] = x_ref[...] + acc


def _out_proj(x, y_sb, y_cv, y_sg, w, *, tm, tn):
    t, d = x.shape
    k = w.shape[0]
    return pl.pallas_call(
        _outproj_kernel,
        grid=(t // tm, d // tn),
        in_specs=[
            pl.BlockSpec((tm, tn), lambda i, j: (i, j)),
            pl.BlockSpec((tm, SB_WIDTH), lambda i, j: (i, 0)),
            pl.BlockSpec((tm, CONV_WIDTH), lambda i, j: (i, 0)),
            pl.BlockSpec((tm, SG_WIDTH), lambda i, j: (i, 0)),
            pl.BlockSpec((k, tn), lambda i, j: (0, j)),
        ],
        out_specs=pl.BlockSpec((tm, tn), lambda i, j: (i, j)),
        out_shape=jax.ShapeDtypeStruct((t, d), F32),
        compiler_params=_params(2, 56),
        name="out_proj_residual",
    )(x, y_sb, y_cv, y_sg, w)


def _ffn_kernel(x_ref, halo_ref, g_ref, wg_ref, wu_ref, cw_ref, cb_ref, wd_ref, fin_ref,
                o_ref, h_ref, *, tm, tiles_per_seq, final):
    i = pl.program_id(0)
    j = pl.program_id(1)
    hr = BF16_ROWS

    @pl.when(j == 0)
    def _():
        x = x_ref[...]
        g = g_ref[...]
        h_ref[hr:, :] = _rms(x, g).astype(BF16)
        first = (i % tiles_per_seq) == 0
        hh = jnp.where(first, 0.0, _rms(halo_ref[...], g))
        h_ref[0:hr, :] = jnp.concatenate([jnp.zeros_like(hh), hh], axis=0).astype(BF16)
        o_ref[...] = x

    gate = jnp.dot(h_ref[...], wg_ref[...], preferred_element_type=F32)
    up = jnp.dot(h_ref[hr:, :], wu_ref[...], preferred_element_type=F32)
    gc = cb_ref[...] + gate[hr:, :] * cw_ref[2:3, :]
    gc = gc + gate[hr - 1:hr - 1 + tm, :] * cw_ref[1:2, :]
    gc = gc + gate[hr - 2:hr - 2 + tm, :] * cw_ref[0:1, :]
    act = (gc * jax.nn.sigmoid(gc) * up).astype(BF16)
    o_ref[...] += jnp.dot(act, wd_ref[...], preferred_element_type=F32)

    if final:
        @pl.when(j == pl.num_programs(1) - 1)
        def _():
            o_ref[...] = _rms(o_ref[...], fin_ref[...])


def _ffn(x, gain, wg, wu, cw, cb, wd, fin, *, seq, tm, fc, final):
    t, d = x.shape
    f = wg.shape[1]
    halo_blocks = tm // SUBLANES
    return pl.pallas_call(
        functools.partial(_ffn_kernel, tm=tm, tiles_per_seq=seq // tm, final=final),
        grid=(t // tm, f // fc),
        in_specs=[
            pl.BlockSpec((tm, d), lambda i, j: (i, 0)),
            pl.BlockSpec((SUBLANES, d), lambda i, j: (jnp.maximum(i * halo_blocks - 1, 0), 0)),
            pl.BlockSpec((1, d), lambda i, j: (0, 0)),
            pl.BlockSpec((d, fc), lambda i, j: (0, j)),
            pl.BlockSpec((d, fc), lambda i, j: (0, j)),
            pl.BlockSpec((FFN_CONV_KERNEL, fc), lambda i, j: (0, j)),
            pl.BlockSpec((1, fc), lambda i, j: (0, j)),
            pl.BlockSpec((fc, d), lambda i, j: (j, 0)),
            pl.BlockSpec((1, d), lambda i, j: (0, 0)),
        ],
        out_specs=pl.BlockSpec((tm, d), lambda i, j: (i, 0)),
        out_shape=jax.ShapeDtypeStruct((t, d), F32),
        scratch_shapes=[pltpu.VMEM((BF16_ROWS + tm, d), BF16)],
        compiler_params=_params(2, 56),
        name="gated_conv_ffn",
    )(x, x, gain.reshape(1, d), wg, wu, cw, cb.reshape(1, f), wd, fin.reshape(1, d))


def _pick(n, pref):
    t = min(pref, n)
    while n % t:
        t //= 2
    return t


def _cast_kernel(w_ref, o_ref):
    o_ref[...] = w_ref[...].astype(o_ref.dtype)


def _first_layer_to_bf16(w, *, tr=512):
    _, k, n = w.shape
    tr = _pick(k, tr)
    return pl.pallas_call(
        _cast_kernel,
        grid=(k // tr,),
        in_specs=[pl.BlockSpec((None, tr, n), lambda i: (0, i, 0))],
        out_specs=pl.BlockSpec((tr, n), lambda i: (i, 0)),
        out_shape=jax.ShapeDtypeStruct((k, n), BF16),
        compiler_params=_params(1, 48),
        name="weights_to_bf16",
    )(w)


def kernel(x, mix_norm, w_in, conv_w, conv_b, conv_ln_g, conv_ln_b, sg_v_norm, sg_w, sg_b,
           merge_norm, w_out, ffn_norm, w_gate, w_up, ffn_conv_w, ffn_conv_b, w_down, final_norm):
    b, s, d = x.shape
    t = b * s
    depth = w_in.shape[0]
    xf = x.reshape(t, d)
    c1 = SB_WIDTH
    c2 = SB_WIDTH + CONV_WIDTH
    w_in_b = _first_layer_to_bf16(w_in)
    for l in range(depth):
        qkv, rest = _proj(xf, mix_norm[l], w_in_b, tm=_pick(t, 512))
        mg = merge_norm[l]
        cast_jobs = [(w_out, l), (w_gate, l), (w_up, l), (w_down, l)]
        if l + 1 < depth:
            cast_jobs.append((w_in, l + 1))
        y_sb, cast = _attention(qkv.reshape(b, s, -1), mg[:c1], cast_jobs, tq=_pick(s, 256),
                                tk=_pick(s, 256), npairs=4)
        w_out_b, w_gate_b, w_up_b, w_down_b = cast[:4]
        if l + 1 < depth:
            w_in_b = cast[4]
        y_cv = _conformer_conv(rest.reshape(b, s, -1), conv_w[l], conv_b[l], conv_ln_g[l],
                               conv_ln_b[l], mg[c1:c2])
        b_full = jnp.broadcast_to(sg_b[l][:, :, None], (SG_HEADS, SG_CHUNK, SG_HEAD_DIM))
        y_sg = _spatial_gating(rest, sg_v_norm[l], sg_w[l], b_full, mg[c2:], tm=_pick(s, 512))
        x1 = _out_proj(xf, y_sb.reshape(t, c1), y_cv.reshape(t, CONV_WIDTH), y_sg,
                       w_out_b, tm=_pick(t, 512), tn=d)
        xf = _ffn(x1, ffn_norm[l], w_gate_b, w_up_b, ffn_conv_w[l], ffn_conv_b[l], w_down_b,
                  final_norm, seq=s, tm=_pick(s, 1024), fc=512, final=(l == depth - 1))
    return xf.reshape(b, s, d)
```

```python
import functools

import jax
import jax.numpy as jnp
from jax import lax
from jax.experimental import pallas as pl
from jax.experimental.pallas import tpu as pltpu

F32 = jnp.float32
BF16 = jnp.bfloat16

EPS = 1e-6
SB_HEAD_DIM = 64
SB_HEADS = 16
SB_WIDTH = SB_HEAD_DIM * SB_HEADS
CONV_GROUP_DIM = 64
CONV_WIDTH = 512
CONV_KERNEL = 31
SG_HEAD_DIM = 128
SG_HEADS = 4
SG_WIDTH = SG_HEAD_DIM * SG_HEADS
SG_CHUNK = 128
STREAM_CHUNK = 64
FFN_CONV_KERNEL = 3

CARRY_CUTOFF = 128.0

LANES = 128
SUBLANES = 8
BF16_ROWS = 16
V7X_VMEM_BYTES = 64 * 1024 * 1024


def _params(n_grid, vmem_mb):
    return pltpu.CompilerParams(
        dimension_semantics=("arbitrary",) * n_grid,
        vmem_limit_bytes=min(vmem_mb * 1024 * 1024, V7X_VMEM_BYTES - 4 * 1024 * 1024),
    )


def _rms(x, g):
    ms = jnp.mean(x * x, axis=-1, keepdims=True)
    return x * lax.rsqrt(ms + EPS) * g


def _half_lane_mean(x, low):
    s_low = jnp.sum(jnp.where(low, x, 0.0), axis=-1, keepdims=True)
    s_all = jnp.sum(x, axis=-1, keepdims=True)
    return jnp.where(low, s_low, s_all - s_low) * (1.0 / 64.0)


def _proj_kernel(x_ref, g_ref, w_ref, qkv_ref, rest_ref):
    n_qkv = qkv_ref.shape[1]
    h = _rms(x_ref[...], g_ref[...]).astype(BF16)
    qkv_ref[...] = jnp.dot(h, w_ref[:, :n_qkv], preferred_element_type=F32).astype(BF16)
    rest_ref[...] = jnp.dot(h, w_ref[:, n_qkv:], preferred_element_type=F32)


def _proj(x, gain, w, *, tm):
    t, d = x.shape
    n_out = w.shape[1]
    n_qkv = 3 * SB_WIDTH
    n_rest = n_out - n_qkv
    return pl.pallas_call(
        _proj_kernel,
        grid=(t // tm,),
        in_specs=[
            pl.BlockSpec((tm, d), lambda i: (i, 0)),
            pl.BlockSpec((1, d), lambda i: (0, 0)),
            pl.BlockSpec((d, n_out), lambda i: (0, 0), pipeline_mode=pl.Buffered(1)),
        ],
        out_specs=[
            pl.BlockSpec((tm, n_qkv), lambda i: (i, 0)),
            pl.BlockSpec((tm, n_rest), lambda i: (i, 0)),
        ],
        out_shape=[
            jax.ShapeDtypeStruct((t, n_qkv), BF16),
            jax.ShapeDtypeStruct((t, n_rest), F32),
        ],
        compiler_params=_params(1, 60),
        name="rms_in_proj",
    )(x, gain.reshape(1, d), w)


def _attn_kernel(q_ref, k_ref, v_ref, g_ref, *refs, tq, tk, npairs, n_cast):
    cast_in = refs[:n_cast]
    o_ref = refs[n_cast]
    cast_out = refs[n_cast + 1:2 * n_cast + 1]
    acc_ref, car_ref = refs[2 * n_cast + 1:]
    for src, dst in zip(cast_in, cast_out):
        dst[...] = src[...].astype(dst.dtype)

    qi = pl.program_id(2)
    n_diag = tq // tk
    nh = 2 * npairs
    low = lax.broadcasted_iota(jnp.int32, (tq, LANES), 1) < SB_HEAD_DIM
    scale = SB_HEAD_DIM ** -0.5
    lanes = [slice(p * LANES, (p + 1) * LANES) for p in range(npairs)]
    q_heads = []
    for p in range(npairs):
        q = q_ref[:, lanes[p]]
        zero = jnp.zeros_like(q)
        q_heads += [jnp.where(low, q, zero) * scale, jnp.where(low, zero, q) * scale]
    tri = (lax.broadcasted_iota(jnp.int32, (tk, tk), 0)
           >= lax.broadcasted_iota(jnp.int32, (tk, tk), 1)).astype(BF16)
    row = lax.broadcasted_iota(jnp.int32, (tq, tk), 0)
    col = lax.broadcasted_iota(jnp.int32, (tq, tk), 1)
    sign = jnp.uint32(0x80000000)

    acc_ref[...] = jnp.zeros_like(acc_ref)
    car_ref[...] = jnp.zeros_like(car_ref)

    def chunk(start, causal):
        kcs = [k_ref[pl.ds(start, tk), lanes[p]] for p in range(npairs)]
        vcs = [v_ref[pl.ds(start, tk), lanes[p]] for p in range(npairs)]
        zs = [lax.dot_general(q_heads[h], kcs[h // 2], (((1,), (1,)), ((), ())),
                              preferred_element_type=F32) for h in range(nh)]
        sps = []
        for z in zs:
            neg_abs = pltpu.bitcast(pltpu.bitcast(z, jnp.uint32) | sign, F32)
            sp = jnp.maximum(z, 0.0) + jnp.log(1.0 + jnp.exp(neg_abs))
            if causal is not None:
                sp = jnp.where(causal, sp, 0.0)
            sps.append(sp)
        withins = [jnp.dot(sp.astype(BF16), tri, preferred_element_type=F32) for sp in sps]
        ws = []
        for h in range(nh):
            car = car_ref[h]
            total = jnp.concatenate([car] * (tk // LANES), axis=-1) + withins[h]
            w = jnp.exp(zs[h] - total)
            if causal is not None:
                w = jnp.where(causal, w, 0.0)
            ws.append(w.astype(BF16))
            car_ref[h] = car + jnp.sum(sps[h], axis=-1, keepdims=True)
        for h in range(nh):
            acc_ref[h] += jnp.dot(ws[h], vcs[h // 2], preferred_element_type=F32)

    for d in range(n_diag - 1, -1, -1):
        chunk(pl.multiple_of(qi * tq + d * tk, tk), (d * tk + col) < row)

    n_off = qi * n_diag

    def more(state):
        it, smallest = state
        return jnp.logical_and(it < n_off, smallest <= CARRY_CUTOFF)

    def body(state):
        it, _ = state
        chunk(pl.multiple_of((n_off - 1 - it) * tk, tk), None)
        car = car_ref[0]
        for h in range(1, nh):
            car = jnp.minimum(car, car_ref[h])
        smallest = jnp.min(jnp.min(car, axis=0, keepdims=True), axis=1, keepdims=True)
        return it + 1, smallest[0, 0]

    lax.while_loop(more, body, (jnp.int32(0), jnp.float32(0.0)))

    for p in range(npairs):
        y = jnp.where(low, acc_ref[2 * p], acc_ref[2 * p + 1])
        ms = _half_lane_mean(y * y, low)
        o_ref[:, lanes[p]] = (y * lax.rsqrt(ms + EPS) * g_ref[:, lanes[p]]).astype(o_ref.dtype)


def _attention(qkv, gain, cast_jobs, *, tq, tk, npairs):
    b, s, _ = qkv.shape
    width = LANES * npairs
    n_groups = SB_WIDTH // width
    n_q = s // tq
    n_steps = b * n_groups * n_q

    def step(bi, hg, qi):
        return (bi * n_groups + hg) * n_q + qi

    cast_specs_in, cast_specs_out, cast_shapes = [], [], []
    for w, layer in cast_jobs:
        _, k, n = w.shape
        n_row = next(a for a in range(n_steps, 0, -1)
                     if n_steps % a == 0 and k % (a * BF16_ROWS) == 0
                     and n % ((n_steps // a) * LANES) == 0)
        n_col = n_steps // n_row
        blk = (k // n_row, n // n_col)
        cast_specs_in.append(pl.BlockSpec(
            (None,) + blk,
            lambda bi, hg, qi, layer=layer, n_col=n_col:
            (layer, step(bi, hg, qi) // n_col, step(bi, hg, qi) % n_col)))
        cast_specs_out.append(pl.BlockSpec(
            blk, lambda bi, hg, qi, n_col=n_col:
            (step(bi, hg, qi) // n_col, step(bi, hg, qi) % n_col)))
        cast_shapes.append(jax.ShapeDtypeStruct((k, n), BF16))

    outs = pl.pallas_call(
        functools.partial(_attn_kernel, tq=tq, tk=tk, npairs=npairs, n_cast=len(cast_jobs)),
        grid=(b, n_groups, n_q),
        in_specs=[
            pl.BlockSpec((None, tq, width), lambda bi, hg, qi: (bi, qi, hg)),
            pl.BlockSpec((None, s, width), lambda bi, hg, qi: (bi, 0, n_groups + hg)),
            pl.BlockSpec((None, s, width), lambda bi, hg, qi: (bi, 0, 2 * n_groups + hg)),
            pl.BlockSpec((1, width), lambda bi, hg, qi: (0, hg)),
        ] + cast_specs_in,
        out_specs=[pl.BlockSpec((None, tq, width), lambda bi, hg, qi: (bi, qi, hg))]
        + cast_specs_out,
        out_shape=[jax.ShapeDtypeStruct((b, s, SB_WIDTH), BF16)] + cast_shapes,
        scratch_shapes=[
            pltpu.VMEM((2 * npairs, tq, LANES), F32),
            pltpu.VMEM((2 * npairs, tq, LANES), F32),
        ],
        compiler_params=_params(3, 48),
        name="stick_breaking_attn",
    )(qkv, qkv, qkv, gain.reshape(1, -1), *[w for w, _ in cast_jobs])
    return outs[0], list(outs[1:])


def _conv_kernel(a_ref, gate_ref, cw_ref, cb_ref, lg_ref, lb_ref, mg_ref, o_ref, hpad_ref,
                 *, s, tr, pad):
    low = lax.broadcasted_iota(jnp.int32, (tr, LANES), 1) < CONV_GROUP_DIM
    hpad_ref[0:pad, :] = jnp.zeros((pad, LANES), F32)

    def glu(it, carry):
        r0 = pl.multiple_of(it * tr, tr)
        a = a_ref[pl.ds(r0, tr), :]
        gt = gate_ref[pl.ds(r0, tr), :]
        hpad_ref[pl.ds(pad + r0, tr), :] = a * jax.nn.sigmoid(gt)
        return carry

    lax.fori_loop(0, s // tr, glu, 0)

    cb = cb_ref[...]
    lg = lg_ref[...]
    lb = lb_ref[...]
    mg = mg_ref[...]

    def tile(it, carry):
        r0 = pl.multiple_of(it * tr, tr)
        acc = jnp.broadcast_to(cb, (tr, LANES))
        for k in range(CONV_KERNEL):
            off = pad - (CONV_KERNEL - 1) + k
            acc = acc + hpad_ref[pl.ds(r0 + off, tr), :] * cw_ref[pl.ds(k, 1), :]
        mu = _half_lane_mean(acc, low)
        dlt = acc - mu
        var = _half_lane_mean(dlt * dlt, low)
        y = dlt * lax.rsqrt(var + EPS) * lg + lb
        y = y * jax.nn.sigmoid(y)
        ms = _half_lane_mean(y * y, low)
        o_ref[pl.ds(r0, tr), :] = (y * lax.rsqrt(ms + EPS) * mg).astype(o_ref.dtype)
        return carry

    lax.fori_loop(0, s // tr, tile, 0, unroll=4)


def _conformer_conv(rest, cw, cb, lg, lb, mg, *, tr=128, pad=32):
    b, s, _ = rest.shape
    nblk = CONV_WIDTH // LANES
    vec = lambda: pl.BlockSpec((1, LANES), lambda bi, c: (0, c))
    return pl.pallas_call(
        functools.partial(_conv_kernel, s=s, tr=tr, pad=pad),
        grid=(b, nblk),
        in_specs=[
            pl.BlockSpec((None, s, LANES), lambda bi, c: (bi, 0, c)),
            pl.BlockSpec((None, s, LANES), lambda bi, c: (bi, 0, nblk + c)),
            pl.BlockSpec((CONV_KERNEL, LANES), lambda bi, c: (0, c)),
            vec(), vec(), vec(), vec(),
        ],
        out_specs=pl.BlockSpec((None, s, LANES), lambda bi, c: (bi, 0, c)),
        out_shape=jax.ShapeDtypeStruct((b, s, CONV_WIDTH), BF16),
        scratch_shapes=[pltpu.VMEM((s + pad, LANES), F32)],
        compiler_params=_params(2, 32),
        name="conformer_conv",
    )(rest, rest, cw, cb.reshape(1, -1), lg.reshape(1, -1), lb.reshape(1, -1), mg.reshape(1, -1))


def _sg_kernel(u_ref, v_ref, vg_ref, w_ref, b_ref, mg_ref, o_ref, *, tm):
    u = jax.nn.gelu(u_ref[...])
    v = jax.nn.gelu(v_ref[...])
    vg = vg_ref[...]
    mg = mg_ref[...]
    ri = lax.broadcasted_iota(jnp.int32, (SG_CHUNK, SG_CHUNK), 0) // STREAM_CHUNK
    ci = lax.broadcasted_iota(jnp.int32, (SG_CHUNK, SG_CHUNK), 1) // STREAM_CHUNK
    allowed = ci <= ri
    for h in range(SG_HEADS):
        sl = slice(h * SG_HEAD_DIM, (h + 1) * SG_HEAD_DIM)
        vn = _rms(v[:, sl], vg[:, sl]).astype(BF16)
        wh = jnp.where(allowed, w_ref[h], 0.0).astype(BF16)
        for n in range(tm // SG_CHUNK):
            rs = slice(n * SG_CHUNK, (n + 1) * SG_CHUNK)
            mixed = jnp.dot(wh, vn[rs], preferred_element_type=F32) + b_ref[h]
            o_ref[rs, sl] = _rms(u[rs, sl] * mixed, mg[:, sl]).astype(o_ref.dtype)


def _spatial_gating(rest, vg, w, b_full, mg, *, tm):
    t, n_rest = rest.shape
    u_blk = (n_rest - 2 * SG_WIDTH) // SG_WIDTH
    return pl.pallas_call(
        functools.partial(_sg_kernel, tm=tm),
        grid=(t // tm,),
        in_specs=[
            pl.BlockSpec((tm, SG_WIDTH), lambda i: (i, u_blk)),
            pl.BlockSpec((tm, SG_WIDTH), lambda i: (i, u_blk + 1)),
            pl.BlockSpec((1, SG_WIDTH), lambda i: (0, 0)),
            pl.BlockSpec((SG_HEADS, SG_CHUNK, SG_CHUNK), lambda i: (0, 0, 0)),
            pl.BlockSpec((SG_HEADS, SG_CHUNK, SG_HEAD_DIM), lambda i: (0, 0, 0)),
            pl.BlockSpec((1, SG_WIDTH), lambda i: (0, 0)),
        ],
        out_specs=pl.BlockSpec((tm, SG_WIDTH), lambda i: (i, 0)),
        out_shape=jax.ShapeDtypeStruct((t, SG_WIDTH), BF16),
        compiler_params=_params(1, 32),
        name="spatial_gating",
    )(rest, rest, vg.reshape(1, -1), w, b_full, mg.reshape(1, -1))


def _outproj_kernel(x_ref, ysb_ref, ysg_ref, a_ref, gt_ref, ha_ref, hg_ref, cw_ref, cb_ref,
                    lg_ref, lb_ref, mg_ref, w_ref, o_ref, hpad_ref, ycv_ref,
                    *, tm, tr, pad, tiles_per_seq):
    c1 = SB_WIDTH
    c2 = SB_WIDTH + CONV_WIDTH
    acc = jnp.dot(ysb_ref[...], w_ref[0:c1, :], preferred_element_type=F32)
    acc += jnp.dot(ysg_ref[...], w_ref[c2:, :], preferred_element_type=F32)

    first = (pl.program_id(0) % tiles_per_seq) == 0
    keep = jnp.where(first, 0.0, 1.0)
    hpad_ref[0:pad, :] = ha_ref[...] * jax.nn.sigmoid(hg_ref[...]) * keep
    hpad_ref[pad:, :] = a_ref[...] * jax.nn.sigmoid(gt_ref[...])
    low = lax.broadcasted_iota(jnp.int32, (tr, LANES), 1) < CONV_GROUP_DIM
    for blk in range(CONV_WIDTH // LANES):
        cs = slice(blk * LANES, (blk + 1) * LANES)
        lg, lb, mg = lg_ref[:, cs], lb_ref[:, cs], mg_ref[:, cs]
        for r0 in range(0, tm, tr):
            conv = jnp.broadcast_to(cb_ref[:, cs], (tr, LANES))
            for k in range(CONV_KERNEL):
                off = r0 + pad - (CONV_KERNEL - 1) + k
                conv = conv + hpad_ref[off:off + tr, cs] * cw_ref[k:k + 1, cs]
            mu = _half_lane_mean(conv, low)
            dlt = conv - mu
            var = _half_lane_mean(dlt * dlt, low)
            y = dlt * lax.rsqrt(var + EPS) * lg + lb
            y = y * jax.nn.sigmoid(y)
            ms = _half_lane_mean(y * y, low)
            ycv_ref[r0:r0 + tr, cs] = (y * lax.rsqrt(ms + EPS) * mg).astype(ycv_ref.dtype)

    acc += jnp.dot(ycv_ref[...], w_ref[c1:c2, :], preferred_element_type=F32)
    o_ref[...] = x_ref[...] + acc


def _out_proj(x, y_sb, y_sg, rest, cw, cb, lg, lb, mg, w, *, seq, tm, tr=128, pad=32):
    t, d = x.shape
    k = w.shape[0]
    halo_blocks = tm // pad
    prev = lambda i: jnp.maximum(i * halo_blocks - 1, 0)
    vec = lambda: pl.BlockSpec((1, CONV_WIDTH), lambda i: (0, 0))
    return pl.pallas_call(
        functools.partial(_outproj_kernel, tm=tm, tr=tr, pad=pad, tiles_per_seq=seq // tm),
        grid=(t // tm,),
        in_specs=[
            pl.BlockSpec((tm, d), lambda i: (i, 0)),
            pl.BlockSpec((tm, SB_WIDTH), lambda i: (i, 0)),
            pl.BlockSpec((tm, SG_WIDTH), lambda i: (i, 0)),
            pl.BlockSpec((tm, CONV_WIDTH), lambda i: (i, 0)),
            pl.BlockSpec((tm, CONV_WIDTH), lambda i: (i, 1)),
            pl.BlockSpec((pad, CONV_WIDTH), lambda i: (prev(i), 0)),
            pl.BlockSpec((pad, CONV_WIDTH), lambda i: (prev(i), 1)),
            pl.BlockSpec((CONV_KERNEL, CONV_WIDTH), lambda i: (0, 0)),
            vec(), vec(), vec(), vec(),
            pl.BlockSpec((k, d), lambda i: (0, 0)),
        ],
        out_specs=pl.BlockSpec((tm, d), lambda i: (i, 0)),
        out_shape=jax.ShapeDtypeStruct((t, d), F32),
        scratch_shapes=[
            pltpu.VMEM((pad + tm, CONV_WIDTH), F32),
            pltpu.VMEM((tm, CONV_WIDTH), BF16),
        ],
        compiler_params=_params(1, 56),
        name="out_proj_conv_residual",
    )(x, y_sb, y_sg, rest, rest, rest, rest, cw, cb.reshape(1, -1), lg.reshape(1, -1),
      lb.reshape(1, -1), mg.reshape(1, -1), w)


def _ffn_kernel(x_ref, halo_ref, g_ref, wg_ref, wu_ref, cw_ref, cb_ref, wd_ref, fin_ref,
                o_ref, h_ref, *, tm, tiles_per_seq, final):
    i = pl.program_id(0)
    j = pl.program_id(1)
    hr = BF16_ROWS

    @pl.when(j == 0)
    def _():
        x = x_ref[...]
        g = g_ref[...]
        h_ref[hr:, :] = _rms(x, g).astype(BF16)
        first = (i % tiles_per_seq) == 0
        hh = jnp.where(first, 0.0, _rms(halo_ref[...], g))
        h_ref[0:hr, :] = jnp.concatenate([jnp.zeros_like(hh), hh], axis=0).astype(BF16)
        o_ref[...] = x

    gate = jnp.dot(h_ref[...], wg_ref[...], preferred_element_type=F32)
    up = jnp.dot(h_ref[hr:, :], wu_ref[...], preferred_element_type=F32)
    gc = cb_ref[...] + gate[hr:, :] * cw_ref[2:3, :]
    gc = gc + gate[hr - 1:hr - 1 + tm, :] * cw_ref[1:2, :]
    gc = gc + gate[hr - 2:hr - 2 + tm, :] * cw_ref[0:1, :]
    act = (gc * jax.nn.sigmoid(gc) * up).astype(BF16)
    o_ref[...] += jnp.dot(act, wd_ref[...], preferred_element_type=F32)

    if final:
        @pl.when(j == pl.num_programs(1) - 1)
        def _():
            o_ref[...] = _rms(o_ref[...], fin_ref[...])


def _ffn(x, gain, wg, wu, cw, cb, wd, fin, *, seq, tm, fc, final):
    t, d = x.shape
    f = wg.shape[1]
    halo_blocks = tm // SUBLANES
    return pl.pallas_call(
        functools.partial(_ffn_kernel, tm=tm, tiles_per_seq=seq // tm, final=final),
        grid=(t // tm, f // fc),
        in_specs=[
            pl.BlockSpec((tm, d), lambda i, j: (i, 0)),
            pl.BlockSpec((SUBLANES, d), lambda i, j: (jnp.maximum(i * halo_blocks - 1, 0), 0)),
            pl.BlockSpec((1, d), lambda i, j: (0, 0)),
            pl.BlockSpec((d, fc), lambda i, j: (0, j)),
            pl.BlockSpec((d, fc), lambda i, j: (0, j)),
            pl.BlockSpec((FFN_CONV_KERNEL, fc), lambda i, j: (0, j)),
            pl.BlockSpec((1, fc), lambda i, j: (0, j)),
            pl.BlockSpec((fc, d), lambda i, j: (j, 0)),
            pl.BlockSpec((1, d), lambda i, j: (0, 0)),
        ],
        out_specs=pl.BlockSpec((tm, d), lambda i, j: (i, 0)),
        out_shape=jax.ShapeDtypeStruct((t, d), F32),
        scratch_shapes=[pltpu.VMEM((BF16_ROWS + tm, d), BF16)],
        compiler_params=_params(2, 56),
        name="gated_conv_ffn",
    )(x, x, gain.reshape(1, d), wg, wu, cw, cb.reshape(1, f), wd, fin.reshape(1, d))


def _pick(n, pref):
    t = min(pref, n)
    while n % t:
        t //= 2
    return t


def _cast_kernel(w_ref, o_ref):
    o_ref[...] = w_ref[...].astype(o_ref.dtype)


def _first_layer_to_bf16(w, *, tr=512):
    _, k, n = w.shape
    tr = _pick(k, tr)
    return pl.pallas_call(
        _cast_kernel,
        grid=(k // tr,),
        in_specs=[pl.BlockSpec((None, tr, n), lambda i: (0, i, 0))],
        out_specs=pl.BlockSpec((tr, n), lambda i: (i, 0)),
        out_shape=jax.ShapeDtypeStruct((k, n), BF16),
        compiler_params=_params(1, 48),
        name="weights_to_bf16",
    )(w)


def kernel(x, mix_norm, w_in, conv_w, conv_b, conv_ln_g, conv_ln_b, sg_v_norm, sg_w, sg_b,
           merge_norm, w_out, ffn_norm, w_gate, w_up, ffn_conv_w, ffn_conv_b, w_down, final_norm):
    b, s, d = x.shape
    t = b * s
    depth = w_in.shape[0]
    xf = x.reshape(t, d)
    c1 = SB_WIDTH
    c2 = SB_WIDTH + CONV_WIDTH
    w_in_b = _first_layer_to_bf16(w_in)
    for l in range(depth):
        qkv, rest = _proj(xf, mix_norm[l], w_in_b, tm=_pick(t, 512))
        mg = merge_norm[l]
        cast_jobs = [(w_out, l), (w_gate, l), (w_up, l), (w_down, l)]
        if l + 1 < depth:
            cast_jobs.append((w_in, l + 1))
        y_sb, cast = _attention(qkv.reshape(b, s, -1), mg[:c1], cast_jobs, tq=_pick(s, 256),
                                tk=_pick(s, 256), npairs=4)
        w_out_b, w_gate_b, w_up_b, w_down_b = cast[:4]
        if l + 1 < depth:
            w_in_b = cast[4]
        b_full = jnp.broadcast_to(sg_b[l][:, :, None], (SG_HEADS, SG_CHUNK, SG_HEAD_DIM))
        y_sg = _spatial_gating(rest, sg_v_norm[l], sg_w[l], b_full, mg[c2:], tm=_pick(s, 512))
        x1 = _out_proj(xf, y_sb.reshape(t, c1), y_sg, rest, conv_w[l], conv_b[l], conv_ln_g[l],
                       conv_ln_b[l], mg[c1:c2], w_out_b, seq=s, tm=_pick(s, 512))
        xf = _ffn(x1, ffn_norm[l], w_gate_b, w_up_b, ffn_conv_w[l], ffn_conv_b[l], w_down_b,
                  final_norm, seq=s, tm=_pick(s, 1024), fc=512, final=(l == depth - 1))
    return xf.reshape(b, s, d)
```

```python
import functools

import jax
import jax.numpy as jnp
from jax import lax
from jax.experimental import pallas as pl
from jax.experimental.pallas import tpu as pltpu

F32 = jnp.float32
BF16 = jnp.bfloat16

EPS = 1e-6
SB_HEAD_DIM = 64
SB_HEADS = 16
SB_WIDTH = SB_HEAD_DIM * SB_HEADS
CONV_GROUP_DIM = 64
CONV_WIDTH = 512
CONV_KERNEL = 31
SG_HEAD_DIM = 128
SG_HEADS = 4
SG_WIDTH = SG_HEAD_DIM * SG_HEADS
SG_CHUNK = 128
STREAM_CHUNK = 64
FFN_CONV_KERNEL = 3

CARRY_CUTOFF = 128.0

LANES = 128
SUBLANES = 8
BF16_ROWS = 16
V7X_VMEM_BYTES = 64 * 1024 * 1024


def _params(n_grid, vmem_mb):
    return pltpu.CompilerParams(
        dimension_semantics=("arbitrary",) * n_grid,
        vmem_limit_bytes=min(vmem_mb * 1024 * 1024, V7X_VMEM_BYTES - 4 * 1024 * 1024),
    )


def _rms(x, g):
    ms = jnp.mean(x * x, axis=-1, keepdims=True)
    return x * lax.rsqrt(ms + EPS) * g


def _half_lane_mean(x, low):
    s_low = jnp.sum(jnp.where(low, x, 0.0), axis=-1, keepdims=True)
    s_all = jnp.sum(x, axis=-1, keepdims=True)
    return jnp.where(low, s_low, s_all - s_low) * (1.0 / 64.0)


def _proj_kernel(x_ref, g_ref, w_ref, qkv_ref, rest_ref):
    n_qkv = qkv_ref.shape[1]
    h = _rms(x_ref[...], g_ref[...]).astype(BF16)
    qkv_ref[...] = jnp.dot(h, w_ref[:, :n_qkv], preferred_element_type=F32).astype(BF16)
    rest_ref[...] = jnp.dot(h, w_ref[:, n_qkv:], preferred_element_type=F32)


def _proj(x, gain, w, *, tm):
    t, d = x.shape
    n_out = w.shape[1]
    n_qkv = 3 * SB_WIDTH
    n_rest = n_out - n_qkv
    return pl.pallas_call(
        _proj_kernel,
        grid=(t // tm,),
        in_specs=[
            pl.BlockSpec((tm, d), lambda i: (i, 0)),
            pl.BlockSpec((1, d), lambda i: (0, 0)),
            pl.BlockSpec((d, n_out), lambda i: (0, 0), pipeline_mode=pl.Buffered(1)),
        ],
        out_specs=[
            pl.BlockSpec((tm, n_qkv), lambda i: (i, 0)),
            pl.BlockSpec((tm, n_rest), lambda i: (i, 0)),
        ],
        out_shape=[
            jax.ShapeDtypeStruct((t, n_qkv), BF16),
            jax.ShapeDtypeStruct((t, n_rest), F32),
        ],
        compiler_params=_params(1, 60),
        name="rms_in_proj",
    )(x, gain.reshape(1, d), w)


def _attn_kernel(q_ref, k_ref, v_ref, g_ref, *refs, tq, tk, npairs, n_cast):
    cast_in = refs[:n_cast]
    o_ref = refs[n_cast]
    cast_out = refs[n_cast + 1:2 * n_cast + 1]
    acc_ref, car_ref = refs[2 * n_cast + 1:]

    qi = pl.program_id(2)
    n_diag = tq // tk
    nh = 2 * npairs
    low = lax.broadcasted_iota(jnp.int32, (tq, LANES), 1) < SB_HEAD_DIM
    scale = SB_HEAD_DIM ** -0.5
    lanes = [slice(p * LANES, (p + 1) * LANES) for p in range(npairs)]
    q_heads = []
    for p in range(npairs):
        q = q_ref[:, lanes[p]]
        zero = jnp.zeros_like(q)
        q_heads += [jnp.where(low, q, zero) * scale, jnp.where(low, zero, q) * scale]
    tri = (lax.broadcasted_iota(jnp.int32, (tk, tk), 0)
           >= lax.broadcasted_iota(jnp.int32, (tk, tk), 1)).astype(BF16)
    row = lax.broadcasted_iota(jnp.int32, (tq, tk), 0)
    col = lax.broadcasted_iota(jnp.int32, (tq, tk), 1)
    sign = jnp.uint32(0x80000000)

    def chunk(start, causal, first=False):
        kcs = [k_ref[pl.ds(start, tk), lanes[p]] for p in range(npairs)]
        vcs = [v_ref[pl.ds(start, tk), lanes[p]] for p in range(npairs)]
        zs = [lax.dot_general(q_heads[h], kcs[h // 2], (((1,), (1,)), ((), ())),
                              preferred_element_type=F32) for h in range(nh)]
        sps = []
        for z in zs:
            neg_abs = pltpu.bitcast(pltpu.bitcast(z, jnp.uint32) | sign, F32)
            sp = jnp.maximum(z, 0.0) + jnp.log(1.0 + jnp.exp(neg_abs))
            if causal is not None:
                sp = jnp.where(causal, sp, 0.0)
            sps.append(sp)
        withins = [jnp.dot(sp.astype(BF16), tri, preferred_element_type=F32) for sp in sps]
        ws = []
        for h in range(nh):
            row_sum = jnp.broadcast_to(jnp.sum(sps[h], axis=-1, keepdims=True), (tq, LANES))
            if first:
                total = withins[h]
                car_ref[h] = row_sum
            else:
                car = car_ref[h]
                total = jnp.concatenate([car] * (tk // LANES), axis=-1) + withins[h]
                car_ref[h] = car + row_sum
            w = jnp.exp(zs[h] - total)
            if causal is not None:
                w = jnp.where(causal, w, 0.0)
            ws.append(w.astype(BF16))
        for h in range(nh):
            pv = jnp.dot(ws[h], vcs[h // 2], preferred_element_type=F32)
            if first:
                acc_ref[h] = pv
            else:
                acc_ref[h] += pv

    for d in range(n_diag - 1, -1, -1):
        chunk(pl.multiple_of(qi * tq + d * tk, tk), (d * tk + col) < row, first=(d == n_diag - 1))

    n_off = qi * n_diag

    def more(state):
        it, smallest = state
        return jnp.logical_and(it < n_off, smallest <= CARRY_CUTOFF)

    def body(state):
        it, _ = state
        chunk(pl.multiple_of((n_off - 1 - it) * tk, tk), None)
        car = car_ref[0]
        for h in range(1, nh):
            car = jnp.minimum(car, car_ref[h])
        smallest = jnp.min(jnp.min(car, axis=0, keepdims=True), axis=1, keepdims=True)
        return it + 1, smallest[0, 0]

    lax.while_loop(more, body, (jnp.int32(0), jnp.float32(0.0)))

    for p in range(npairs):
        y = jnp.where(low, acc_ref[2 * p], acc_ref[2 * p + 1])
        ms = _half_lane_mean(y * y, low)
        o_ref[:, lanes[p]] = (y * lax.rsqrt(ms + EPS) * g_ref[:, lanes[p]]).astype(o_ref.dtype)

    for src, dst in zip(cast_in, cast_out):
        dst[...] = src[...].astype(dst.dtype)


def _attention(qkv, gain, cast_jobs, *, tq, tk, npairs):
    b, s, _ = qkv.shape
    width = LANES * npairs
    n_groups = SB_WIDTH // width
    n_q = s // tq
    n_steps = b * n_groups * n_q

    def step(bi, hg, qi):
        return (bi * n_groups + hg) * n_q + qi

    cast_specs_in, cast_specs_out, cast_shapes = [], [], []
    for w, layer in cast_jobs:
        _, k, n = w.shape
        n_row = next(a for a in range(n_steps, 0, -1)
                     if n_steps % a == 0 and k % (a * BF16_ROWS) == 0
                     and n % ((n_steps // a) * LANES) == 0)
        n_col = n_steps // n_row
        blk = (k // n_row, n // n_col)
        cast_specs_in.append(pl.BlockSpec(
            (None,) + blk,
            lambda bi, hg, qi, layer=layer, n_col=n_col:
            (layer, step(bi, hg, qi) // n_col, step(bi, hg, qi) % n_col)))
        cast_specs_out.append(pl.BlockSpec(
            blk, lambda bi, hg, qi, n_col=n_col:
            (step(bi, hg, qi) // n_col, step(bi, hg, qi) % n_col)))
        cast_shapes.append(jax.ShapeDtypeStruct((k, n), BF16))

    outs = pl.pallas_call(
        functools.partial(_attn_kernel, tq=tq, tk=tk, npairs=npairs, n_cast=len(cast_jobs)),
        grid=(b, n_groups, n_q),
        in_specs=[
            pl.BlockSpec((None, tq, width), lambda bi, hg, qi: (bi, qi, hg)),
            pl.BlockSpec((None, s, width), lambda bi, hg, qi: (bi, 0, n_groups + hg)),
            pl.BlockSpec((None, s, width), lambda bi, hg, qi: (bi, 0, 2 * n_groups + hg)),
            pl.BlockSpec((1, width), lambda bi, hg, qi: (0, hg)),
        ] + cast_specs_in,
        out_specs=[pl.BlockSpec((None, tq, width), lambda bi, hg, qi: (bi, qi, hg))]
        + cast_specs_out,
        out_shape=[jax.ShapeDtypeStruct((b, s, SB_WIDTH), BF16)] + cast_shapes,
        scratch_shapes=[
            pltpu.VMEM((2 * npairs, tq, LANES), F32),
            pltpu.VMEM((2 * npairs, tq, LANES), F32),
        ],
        compiler_params=_params(3, 48),
        name="stick_breaking_attn",
    )(qkv, qkv, qkv, gain.reshape(1, -1), *[w for w, _ in cast_jobs])
    return outs[0], list(outs[1:])


def _conv_kernel(a_ref, gate_ref, cw_ref, cb_ref, lg_ref, lb_ref, mg_ref, o_ref, hpad_ref,
                 *, s, tr, pad):
    low = lax.broadcasted_iota(jnp.int32, (tr, LANES), 1) < CONV_GROUP_DIM
    hpad_ref[0:pad, :] = jnp.zeros((pad, LANES), F32)

    def glu(it, carry):
        r0 = pl.multiple_of(it * tr, tr)
        a = a_ref[pl.ds(r0, tr), :]
        gt = gate_ref[pl.ds(r0, tr), :]
        hpad_ref[pl.ds(pad + r0, tr), :] = a * jax.nn.sigmoid(gt)
        return carry

    lax.fori_loop(0, s // tr, glu, 0)

    cb = cb_ref[...]
    lg = lg_ref[...]
    lb = lb_ref[...]
    mg = mg_ref[...]

    def tile(it, carry):
        r0 = pl.multiple_of(it * tr, tr)
        acc = jnp.broadcast_to(cb, (tr, LANES))
        for k in range(CONV_KERNEL):
            off = pad - (CONV_KERNEL - 1) + k
            acc = acc + hpad_ref[pl.ds(r0 + off, tr), :] * cw_ref[pl.ds(k, 1), :]
        mu = _half_lane_mean(acc, low)
        dlt = acc - mu
        var = _half_lane_mean(dlt * dlt, low)
        y = dlt * lax.rsqrt(var + EPS) * lg + lb
        y = y * jax.nn.sigmoid(y)
        ms = _half_lane_mean(y * y, low)
        o_ref[pl.ds(r0, tr), :] = (y * lax.rsqrt(ms + EPS) * mg).astype(o_ref.dtype)
        return carry

    lax.fori_loop(0, s // tr, tile, 0, unroll=4)


def _conformer_conv(rest, cw, cb, lg, lb, mg, *, tr=128, pad=32):
    b, s, _ = rest.shape
    nblk = CONV_WIDTH // LANES
    vec = lambda: pl.BlockSpec((1, LANES), lambda bi, c: (0, c))
    return pl.pallas_call(
        functools.partial(_conv_kernel, s=s, tr=tr, pad=pad),
        grid=(b, nblk),
        in_specs=[
            pl.BlockSpec((None, s, LANES), lambda bi, c: (bi, 0, c)),
            pl.BlockSpec((None, s, LANES), lambda bi, c: (bi, 0, nblk + c)),
            pl.BlockSpec((CONV_KERNEL, LANES), lambda bi, c: (0, c)),
            vec(), vec(), vec(), vec(),
        ],
        out_specs=pl.BlockSpec((None, s, LANES), lambda bi, c: (bi, 0, c)),
        out_shape=jax.ShapeDtypeStruct((b, s, CONV_WIDTH), BF16),
        scratch_shapes=[pltpu.VMEM((s + pad, LANES), F32)],
        compiler_params=_params(2, 32),
        name="conformer_conv",
    )(rest, rest, cw, cb.reshape(1, -1), lg.reshape(1, -1), lb.reshape(1, -1), mg.reshape(1, -1))


def _sg_kernel(u_ref, v_ref, vg_ref, w_ref, b_ref, mg_ref, o_ref, *, tm):
    u = jax.nn.gelu(u_ref[...])
    v = jax.nn.gelu(v_ref[...])
    vg = vg_ref[...]
    mg = mg_ref[...]
    ri = lax.broadcasted_iota(jnp.int32, (SG_CHUNK, SG_CHUNK), 0) // STREAM_CHUNK
    ci = lax.broadcasted_iota(jnp.int32, (SG_CHUNK, SG_CHUNK), 1) // STREAM_CHUNK
    allowed = ci <= ri
    for h in range(SG_HEADS):
        sl = slice(h * SG_HEAD_DIM, (h + 1) * SG_HEAD_DIM)
        vn = _rms(v[:, sl], vg[:, sl]).astype(BF16)
        wh = jnp.where(allowed, w_ref[h], 0.0).astype(BF16)
        for n in range(tm // SG_CHUNK):
            rs = slice(n * SG_CHUNK, (n + 1) * SG_CHUNK)
            mixed = jnp.dot(wh, vn[rs], preferred_element_type=F32) + b_ref[h]
            o_ref[rs, sl] = _rms(u[rs, sl] * mixed, mg[:, sl]).astype(o_ref.dtype)


def _spatial_gating(rest, vg, w, b_full, mg, *, tm):
    t, n_rest = rest.shape
    u_blk = (n_rest - 2 * SG_WIDTH) // SG_WIDTH
    return pl.pallas_call(
        functools.partial(_sg_kernel, tm=tm),
        grid=(t // tm,),
        in_specs=[
            pl.BlockSpec((tm, SG_WIDTH), lambda i: (i, u_blk)),
            pl.BlockSpec((tm, SG_WIDTH), lambda i: (i, u_blk + 1)),
            pl.BlockSpec((1, SG_WIDTH), lambda i: (0, 0)),
            pl.BlockSpec((SG_HEADS, SG_CHUNK, SG_CHUNK), lambda i: (0, 0, 0)),
            pl.BlockSpec((SG_HEADS, SG_CHUNK, SG_HEAD_DIM), lambda i: (0, 0, 0)),
            pl.BlockSpec((1, SG_WIDTH), lambda i: (0, 0)),
        ],
        out_specs=pl.BlockSpec((tm, SG_WIDTH), lambda i: (i, 0)),
        out_shape=jax.ShapeDtypeStruct((t, SG_WIDTH), BF16),
        compiler_params=_params(1, 32),
        name="spatial_gating",
    )(rest, rest, vg.reshape(1, -1), w, b_full, mg.reshape(1, -1))


def _outproj_kernel(x_ref, ysb_ref, ycv_ref, ysg_ref, w_ref, o_ref):
    c1 = SB_WIDTH
    c2 = SB_WIDTH + CONV_WIDTH
    acc = jnp.dot(ysb_ref[...], w_ref[0:c1, :], preferred_element_type=F32)
    acc += jnp.dot(ycv_ref[...], w_ref[c1:c2, :], preferred_element_type=F32)
    acc += jnp.dot(ysg_ref[...], w_ref[c2:, :], preferred_element_type=F32)
    o_ref[...] = x_ref[...] + acc


def _out_proj(x, y_sb, y_cv, y_sg, w, *, tm, tn):
    t, d = x.shape
    k = w.shape[0]
    return pl.pallas_call(
        _outproj_kernel,
        grid=(t // tm, d // tn),
        in_specs=[
            pl.BlockSpec((tm, tn), lambda i, j: (i, j)),
            pl.BlockSpec((tm, SB_WIDTH), lambda i, j: (i, 0)),
            pl.BlockSpec((tm, CONV_WIDTH), lambda i, j: (i, 0)),
            pl.BlockSpec((tm, SG_WIDTH), lambda i, j: (i, 0)),
            pl.BlockSpec((k, tn), lambda i, j: (0, j)),
        ],
        out_specs=pl.BlockSpec((tm, tn), lambda i, j: (i, j)),
        out_shape=jax.ShapeDtypeStruct((t, d), F32),
        compiler_params=_params(2, 56),
        name="out_proj_residual",
    )(x, y_sb, y_cv, y_sg, w)


def _ffn_kernel(x_ref, halo_ref, g_ref, wg_ref, wu_ref, cw_ref, cb_ref, wd_ref, fin_ref,
                o_ref, h_ref, *, tm, tiles_per_seq, final):
    i = pl.program_id(0)
    j = pl.program_id(1)
    hr = BF16_ROWS

    @pl.when(j == 0)
    def _():
        x = x_ref[...]
        g = g_ref[...]
        h_ref[hr:, :] = _rms(x, g).astype(BF16)
        first = (i % tiles_per_seq) == 0
        hh = jnp.where(first, 0.0, _rms(halo_ref[...], g))
        h_ref[0:hr, :] = jnp.concatenate([jnp.zeros_like(hh), hh], axis=0).astype(BF16)
        o_ref[...] = x

    gate = jnp.dot(h_ref[...], wg_ref[...], preferred_element_type=F32)
    up = jnp.dot(h_ref[hr:, :], wu_ref[...], preferred_element_type=F32)
    gc = cb_ref[...] + gate[hr:, :] * cw_ref[2:3, :]
    gc = gc + gate[hr - 1:hr - 1 + tm, :] * cw_ref[1:2, :]
    gc = gc + gate[hr - 2:hr - 2 + tm, :] * cw_ref[0:1, :]
    act = (gc * jax.nn.sigmoid(gc) * up).astype(BF16)
    o_ref[...] += jnp.dot(act, wd_ref[...], preferred_element_type=F32)

    if final:
        @pl.when(j == pl.num_programs(1) - 1)
        def _():
            o_ref[...] = _rms(o_ref[...], fin_ref[...])


def _ffn(x, gain, wg, wu, cw, cb, wd, fin, *, seq, tm, fc, final):
    t, d = x.shape
    f = wg.shape[1]
    halo_blocks = tm // SUBLANES
    return pl.pallas_call(
        functools.partial(_ffn_kernel, tm=tm, tiles_per_seq=seq // tm, final=final),
        grid=(t // tm, f // fc),
        in_specs=[
            pl.BlockSpec((tm, d), lambda i, j: (i, 0)),
            pl.BlockSpec((SUBLANES, d), lambda i, j: (jnp.maximum(i * halo_blocks - 1, 0), 0)),
            pl.BlockSpec((1, d), lambda i, j: (0, 0)),
            pl.BlockSpec((d, fc), lambda i, j: (0, j)),
            pl.BlockSpec((d, fc), lambda i, j: (0, j)),
            pl.BlockSpec((FFN_CONV_KERNEL, fc), lambda i, j: (0, j)),
            pl.BlockSpec((1, fc), lambda i, j: (0, j)),
            pl.BlockSpec((fc, d), lambda i, j: (j, 0)),
            pl.BlockSpec((1, d), lambda i, j: (0, 0)),
        ],
        out_specs=pl.BlockSpec((tm, d), lambda i, j: (i, 0)),
        out_shape=jax.ShapeDtypeStruct((t, d), F32),
        scratch_shapes=[pltpu.VMEM((BF16_ROWS + tm, d), BF16)],
        compiler_params=_params(2, 56),
        name="gated_conv_ffn",
    )(x, x, gain.reshape(1, d), wg, wu, cw, cb.reshape(1, f), wd, fin.reshape(1, d))


def _pick(n, pref):
    t = min(pref, n)
    while n % t:
        t //= 2
    return t


def _cast_kernel(w_ref, o_ref):
    o_ref[...] = w_ref[...].astype(o_ref.dtype)


def _first_layer_to_bf16(w, *, tr=512):
    _, k, n = w.shape
    tr = _pick(k, tr)
    return pl.pallas_call(
        _cast_kernel,
        grid=(k // tr,),
        in_specs=[pl.BlockSpec((None, tr, n), lambda i: (0, i, 0))],
        out_specs=pl.BlockSpec((tr, n), lambda i: (i, 0)),
        out_shape=jax.ShapeDtypeStruct((k, n), BF16),
        compiler_params=_params(1, 48),
        name="weights_to_bf16",
    )(w)


def kernel(x, mix_norm, w_in, conv_w, conv_b, conv_ln_g, conv_ln_b, sg_v_norm, sg_w, sg_b,
           merge_norm, w_out, ffn_norm, w_gate, w_up, ffn_conv_w, ffn_conv_b, w_down, final_norm):
    b, s, d = x.shape
    t = b * s
    depth = w_in.shape[0]
    xf = x.reshape(t, d)
    c1 = SB_WIDTH
    c2 = SB_WIDTH + CONV_WIDTH
    w_in_b = _first_layer_to_bf16(w_in)
    for l in range(depth):
        qkv, rest = _proj(xf, mix_norm[l], w_in_b, tm=_pick(t, 512))
        mg = merge_norm[l]
        cast_jobs = [(w_out, l), (w_gate, l), (w_up, l), (w_down, l)]
        if l + 1 < depth:
            cast_jobs.append((w_in, l + 1))
        y_sb, cast = _attention(qkv.reshape(b, s, -1), mg[:c1], cast_jobs, tq=_pick(s, 256),
                                tk=_pick(s, 256), npairs=4)
        w_out_b, w_gate_b, w_up_b, w_down_b = cast[:4]
        if l + 1 < depth:
            w_in_b = cast[4]
        y_cv = _conformer_conv(rest.reshape(b, s, -1), conv_w[l], conv_b[l], conv_ln_g[l],
                               conv_ln_b[l], mg[c1:c2])
        b_full = jnp.broadcast_to(sg_b[l][:, :, None], (SG_HEADS, SG_CHUNK, SG_HEAD_DIM))
        y_sg = _spatial_gating(rest, sg_v_norm[l], sg_w[l], b_full, mg[c2:], tm=_pick(s, 512))
        x1 = _out_proj(xf, y_sb.reshape(t, c1), y_cv.reshape(t, CONV_WIDTH), y_sg,
                       w_out_b, tm=_pick(t, 512), tn=d)
        xf = _ffn(x1, ffn_norm[l], w_gate_b, w_up_b, ffn_conv_w[l], ffn_conv_b[l], w_down_b,
                  final_norm, seq=s, tm=_pick(s, 1024), fc=512, final=(l == depth - 1))
    return xf.reshape(b, s, d)
```
